```python
import jax, jax.numpy as jnp
from jax import lax
import numpy as np

D_MODEL = 1024
BATCH = 4
SEQ = 8192
DEPTH = 2

HEAD_DIM = 64
N_HEADS = 4
BRANCH_WIDTH = N_HEADS * HEAD_DIM
N_BRANCHES = 4
Q_BLOCK = 128
RMS_EPS = 1e-6

DSW_PATTERNS = ((128, 1), (512, 4), (2048, 16))

MLA_Q_RANK = 192
MLA_KV_RANK = 128
MLA_NOPE_DIM = 64
MLA_ROPE_DIM = 32
MLA_V_DIM = 64
ROPE_THETA = 10000.0

NSA_CMP_BLOCK = 32
NSA_CMP_STRIDE = 16
NSA_CMP_HIDDEN = 128
NSA_SEL_BLOCK = 64
NSA_TOP_N = 16
NSA_WINDOW = 512
NSA_KV_DIM = 64
NSA_FORCE_SCORE = 1e9

N_ALIBI_HEADS = 8

IN_SPLITS = (
    ("a_q", BRANCH_WIDTH), ("a_k", BRANCH_WIDTH), ("a_v", BRANCH_WIDTH), ("a_gate", BRANCH_WIDTH),
    ("b_cq", MLA_Q_RANK), ("b_ckv", MLA_KV_RANK), ("b_kpe", MLA_ROPE_DIM), ("b_gate", BRANCH_WIDTH),
    ("c_q", BRANCH_WIDTH), ("c_kc", NSA_KV_DIM), ("c_vc", NSA_KV_DIM), ("c_ks", NSA_KV_DIM),
    ("c_vs", NSA_KV_DIM), ("c_kw", NSA_KV_DIM), ("c_vw", NSA_KV_DIM), ("c_g", 3 * N_HEADS),
    ("c_gate", BRANCH_WIDTH),
    ("d_q", BRANCH_WIDTH), ("d_k", BRANCH_WIDTH), ("d_v", BRANCH_WIDTH), ("d_gate", BRANCH_WIDTH),
)
D_IN = sum(w for _, w in IN_SPLITS)

kernel_name = "hybrid_gated_dilated_mla_nsa_stickbreaking"


def rmsnorm(x, g):
    xf = x.astype(jnp.float32)
    y = xf * lax.rsqrt(jnp.mean(xf * xf, axis=-1, keepdims=True) + RMS_EPS)
    return (y * g.astype(jnp.float32)).astype(x.dtype)


def split_columns(p):
    out, off = {}, 0
    for name, w in IN_SPLITS:
        out[name] = p[..., off:off + w]
        off += w
    return out


def alibi_slopes():
    s = 2.0 ** (-8.0 * (np.arange(N_ALIBI_HEADS) + 1) / N_ALIBI_HEADS)
    s = jnp.asarray(s, jnp.float32)
    return s[0::2], s[1::2]


def masked_softmax(s, mask):
    s = jnp.where(mask, s, -jnp.inf)
    m = jnp.max(s, axis=-1, keepdims=True)
    m = jnp.where(jnp.isfinite(m), m, 0.0)
    e = jnp.exp(s - m)
    return e / jnp.maximum(jnp.sum(e, axis=-1, keepdims=True), 1e-30)


def sweep_query_blocks(block_fn, seq):
    out = lax.map(block_fn, jnp.arange(seq // Q_BLOCK))
    nq, b, tq, h, d = out.shape
    return jnp.swapaxes(out, 0, 1).reshape(b, nq * tq, h, d)


def rope(x, pos):
    half = x.shape[-1] // 2
    inv = ROPE_THETA ** (-jnp.arange(half, dtype=jnp.float32) / half)
    ang = pos.astype(jnp.float32)[:, None] * inv[None, :]
    cos, sin = jnp.cos(ang)[:, None, :], jnp.sin(ang)[:, None, :]
    xf = x.astype(jnp.float32)
    x1, x2 = xf[..., :half], xf[..., half:]
    return jnp.concatenate([x1 * cos - x2 * sin, x1 * sin + x2 * cos], axis=-1).astype(x.dtype)


def dilated_window_attention(q, k, v, slopes):
    B, S, H, Dh = q.shape
    scale = Dh ** -0.5

    def block(qb):
        t0 = qb * Q_BLOCK
        t = t0 + jnp.arange(Q_BLOCK)
        qblk = lax.dynamic_slice_in_dim(q, t0, Q_BLOCK, axis=1)
        outs, lses = [], []
        for window, dil in DSW_PATTERNS:
            dist = dil * jnp.arange(window // dil + 1)
            idx = t[:, None] - dist[None, :]
            valid = idx >= 0
            idxc = jnp.maximum(idx, 0)
            kg, vg = k[:, idxc], v[:, idxc]
            s = jnp.einsum('bqhd,bqjhd->bhqj', qblk, kg).astype(jnp.float32) * scale
            s = s - slopes[:, None, None] * dist.astype(jnp.float32)
            s = jnp.where(valid[None, None], s, -jnp.inf)
            lse = jax.nn.logsumexp(s, axis=-1)
            p = jnp.exp(s - lse[..., None])
            outs.append(jnp.einsum('bhqj,bqjhd->bqhd', p.astype(v.dtype), vg))
            lses.append(lse)
        w = jax.nn.softmax(jnp.stack(lses, axis=0), axis=0)
        o = 0.0
        for i in range(len(DSW_PATTERNS)):
            o = o + jnp.transpose(w[i], (0, 2, 1))[..., None].astype(v.dtype) * outs[i]
        return o

    return sweep_query_blocks(block, S)


def blocked_causal_softmax(q, k, v, scale):
    B, S, H, _ = q.shape
    kpos = jnp.arange(S)

    def block(qb):
        t0 = qb * Q_BLOCK
        t = t0 + jnp.arange(Q_BLOCK)
        qblk = lax.dynamic_slice_in_dim(q, t0, Q_BLOCK, axis=1)
        s = jnp.einsum('bqhd,bkhd->bhqk', qblk, k).astype(jnp.float32) * scale
        s = jnp.where(kpos[None, :] <= t[:, None], s, -jnp.inf)
        p = jax.nn.softmax(s, axis=-1)
        return jnp.einsum('bhqk,bkhd->bqhd', p.astype(v.dtype), v)

    return sweep_query_blocks(block, S)


def mla_attention(cq, ckv, kpe, q_norm_g, w_uq, kv_norm_g, w_ukv):
    B, S, _ = cq.shape
    pos = jnp.arange(S)
    q = (rmsnorm(cq, q_norm_g) @ w_uq).reshape(B, S, N_HEADS, MLA_NOPE_DIM + MLA_ROPE_DIM)
    q = jnp.concatenate([q[..., :MLA_NOPE_DIM], rope(q[..., MLA_NOPE_DIM:], pos)], axis=-1)
    kv = (rmsnorm(ckv, kv_norm_g) @ w_ukv).reshape(B, S, N_HEADS, MLA_NOPE_DIM + MLA_V_DIM)
    k_nope, v = kv[..., :MLA_NOPE_DIM], kv[..., MLA_NOPE_DIM:]
    k_pe = rope(kpe[:, :, None, :], pos)
    k = jnp.concatenate([k_nope, jnp.broadcast_to(k_pe, (B, S, N_HEADS, MLA_ROPE_DIM))], axis=-1)
    return blocked_causal_softmax(q, k, v, (MLA_NOPE_DIM + MLA_ROPE_DIM) ** -0.5)


def compress_blocks(tok, pos_emb, w1, b1, w2, b2):
    B, S, Dk = tok.shape
    n_cmp = (S - NSA_CMP_BLOCK) // NSA_CMP_STRIDE + 1
    idx = np.arange(n_cmp)[:, None] * NSA_CMP_STRIDE + np.arange(NSA_CMP_BLOCK)[None, :]
    blocks = tok[:, idx] + pos_emb
    hid = jax.nn.gelu(blocks.reshape(B, n_cmp, NSA_CMP_BLOCK * Dk) @ w1 + b1)
    return hid @ w2 + b2


def nsa_attention(q, kc, vc, ks, vs, kw, vw, gates, pos_emb, w1, b1, w2, b2, slopes):
    B, S, H, Dh = q.shape
    scale = Dh ** -0.5
    k_cmp = compress_blocks(kc, pos_emb[0], w1[0], b1[0], w2[0], b2[0])
    v_cmp = compress_blocks(vc, pos_emb[1], w1[1], b1[1], w2[1], b2[1])
    n_cmp = k_cmp.shape[1]
    cmp_end = jnp.arange(n_cmp) * NSA_CMP_STRIDE + NSA_CMP_BLOCK - 1
    n_sel = S // NSA_SEL_BLOCK
    top_n = min(NSA_TOP_N, n_sel)
    r_sel, r_cmp = NSA_SEL_BLOCK // NSA_CMP_STRIDE, NSA_CMP_BLOCK // NSA_CMP_STRIDE
    src = (jnp.arange(n_sel)[:, None, None] * r_sel - jnp.arange(r_sel)[None, :, None]
           - jnp.arange(r_cmp)[None, None, :]).reshape(n_sel, -1)
    sel_map = jnp.sum(src[..., None] == jnp.arange(n_cmp), axis=1).astype(jnp.float32)
    kw_pad = jnp.pad(kw, ((0, 0), (NSA_WINDOW, 0), (0, 0)))
    vw_pad = jnp.pad(vw, ((0, 0), (NSA_WINDOW, 0), (0, 0)))
    blk = jnp.arange(n_sel)
    sel_off = jnp.arange(NSA_SEL_BLOCK)

    def block(qb):
        t0 = qb * Q_BLOCK
        t = t0 + jnp.arange(Q_BLOCK)
        qblk = lax.dynamic_slice_in_dim(q, t0, Q_BLOCK, axis=1)
        gblk = lax.dynamic_slice_in_dim(gates, t0, Q_BLOCK, axis=1)
        dist_c = (t[:, None] - cmp_end[None, :]).astype(jnp.float32)
        s = jnp.einsum('bqhd,bnd->bhqn', qblk, k_cmp).astype(jnp.float32) * scale
        p_cmp = masked_softmax(s - slopes[:, None, None] * dist_c, dist_c >= 0)
        o_cmp = jnp.einsum('bhqn,bnd->bqhd', p_cmp.astype(v_cmp.dtype), v_cmp)
        p_sel = jnp.einsum('bhqn,jn->bqj', p_cmp, sel_map)
        cur = t // NSA_SEL_BLOCK
        valid = blk[None, :] * NSA_SEL_BLOCK <= t[:, None]
        forced = (blk[None, :] == 0) | (blk[None, :] == cur[:, None]) | (blk[None, :] == cur[:, None] - 1)
        score = jnp.where(valid, jnp.where(forced, NSA_FORCE_SCORE, p_sel), -jnp.inf)
        _, sel = lax.top_k(score, top_n)
        tok = (sel[..., None] * NSA_SEL_BLOCK + sel_off).reshape(B, Q_BLOCK, top_n * NSA_SEL_BLOCK)
        ks_g = jax.vmap(lambda a, i: a[i])(ks, tok)
        vs_g = jax.vmap(lambda a, i: a[i])(vs, tok)
        dist_s = (t[None, :, None] - tok).astype(jnp.float32)
        s = jnp.einsum('bqhd,bqkd->bhqk', qblk, ks_g).astype(jnp.float32) * scale
        s = s - slopes[None, :, None, None] * dist_s[:, None]
        p = masked_softmax(s, (dist_s >= 0)[:, None])
        o_slc = jnp.einsum('bhqk,bqkd->bqhd', p.astype(vs_g.dtype), vs_g)
        kwin = lax.dynamic_slice_in_dim(kw_pad, t0, Q_BLOCK + NSA_WINDOW, axis=1)
        vwin = lax.dynamic_slice_in_dim(vw_pad, t0, Q_BLOCK + NSA_WINDOW, axis=1)
        spos = t0 - NSA_WINDOW + jnp.arange(Q_BLOCK + NSA_WINDOW)
        dist_w = t[:, None] - spos[None, :]
        mask_w = (dist_w >= 0) & (dist_w < NSA_WINDOW) & (spos[None, :] >= 0)
        s = jnp.einsum('bqhd,bkd->bhqk', qblk, kwin).astype(jnp.float32) * scale
        p = masked_softmax(s - slopes[:, None, None] * dist_w.astype(jnp.float32), mask_w)
        o_win = jnp.einsum('bhqk,bkd->bqhd', p.astype(vwin.dtype), vwin)
        return gblk[..., 0:1] * o_cmp + gblk[..., 1:2] * o_slc + gblk[..., 2:3] * o_win

    return sweep_query_blocks(block, S)


def stick_breaking_attention(q, k, v):
    B, S, H, Dh = q.shape
    scale = Dh ** -0.5
    kpos = jnp.arange(S)

    def block(qb):
        t0 = qb * Q_BLOCK
        t = t0 + jnp.arange(Q_BLOCK)
        qblk = lax.dynamic_slice_in_dim(q, t0, Q_BLOCK, axis=1)
        z = jnp.einsum('bqhd,bkhd->bhqk', qblk, k).astype(jnp.float32) * scale
        mask = kpos[None, :] < t[:, None]
        log1m = jnp.where(mask, jax.nn.log_sigmoid(-z), 0.0)
        after = lax.cumsum(log1m, axis=3, reverse=True) - log1m
        a = jnp.where(mask, jnp.exp(jax.nn.log_sigmoid(z) + after), 0.0)
        return jnp.einsum('bhqk,bkhd->bqhd', a.astype(v.dtype), v)

    return sweep_query_blocks(block, S)


def setup_inputs(seed: int = 0) -> dict:
    key = jax.random.key(seed)
    ks = jax.random.split(key, 18)
    f32 = jnp.float32

    def nrm(k, shape, fan_in):
        return jax.random.normal(k, shape, f32) * (fan_in ** -0.5)

    def gain(k, shape):
        return 1.0 + 0.05 * jax.random.normal(k, shape, f32)

    return {
        "x": jax.random.normal(ks[0], (BATCH, SEQ, D_MODEL), f32),
        "norm_g": gain(ks[1], (DEPTH, D_MODEL)),
        "w_in": nrm(ks[2], (DEPTH, D_MODEL, D_IN), D_MODEL),
        "mla_q_norm": gain(ks[3], (DEPTH, MLA_Q_RANK)),
        "mla_w_uq": nrm(ks[4], (DEPTH, MLA_Q_RANK, N_HEADS * (MLA_NOPE_DIM + MLA_ROPE_DIM)), MLA_Q_RANK),
        "mla_kv_norm": gain(ks[5], (DEPTH, MLA_KV_RANK)),
        "mla_w_ukv": nrm(ks[6], (DEPTH, MLA_KV_RANK, N_HEADS * (MLA_NOPE_DIM + MLA_V_DIM)), MLA_KV_RANK),
        "nsa_pos": 0.1 * jax.random.normal(ks[7], (DEPTH, 2, NSA_CMP_BLOCK, NSA_KV_DIM), f32),
        "nsa_w1": nrm(ks[8], (DEPTH, 2, NSA_CMP_BLOCK * NSA_KV_DIM, NSA_CMP_HIDDEN), NSA_CMP_BLOCK * NSA_KV_DIM),
        "nsa_b1": 0.02 * jax.random.normal(ks[9], (DEPTH, 2, NSA_CMP_HIDDEN), f32),
        "nsa_w2": nrm(ks[10], (DEPTH, 2, NSA_CMP_HIDDEN, NSA_KV_DIM), NSA_CMP_HIDDEN),
        "nsa_b2": 0.02 * jax.random.normal(ks[11], (DEPTH, 2, NSA_KV_DIM), f32),
        "w_branch": nrm(ks[12], (DEPTH, N_BRANCHES, BRANCH_WIDTH, D_MODEL), BRANCH_WIDTH),
        "w_merge": nrm(ks[13], (DEPTH, N_BRANCHES, D_MODEL, D_MODEL), D_MODEL),
        "b_merge": 0.02 * jax.random.normal(ks[14], (DEPTH, N_BRANCHES, D_MODEL), f32),
        "w_out": nrm(ks[15], (DEPTH, D_MODEL, D_MODEL), D_MODEL),
        "final_norm_g": gain(ks[16], (D_MODEL,)),
    }


def reference(x, norm_g, w_in, mla_q_norm, mla_w_uq, mla_kv_norm, mla_w_ukv, nsa_pos, nsa_w1,
              nsa_b1, nsa_w2, nsa_b2, w_branch, w_merge, b_merge, w_out, final_norm_g):
    B, S, D = x.shape
    slopes_a, slopes_c = alibi_slopes()
    for l in range(DEPTH):
        h = rmsnorm(x, norm_g[l])
        c = split_columns(h @ w_in[l])
        heads = lambda a: a.reshape(B, S, N_HEADS, HEAD_DIM)
        flat = lambda a: a.reshape(B, S, BRANCH_WIDTH)
        y_a = flat(dilated_window_attention(heads(c["a_q"]), heads(c["a_k"]), heads(c["a_v"]), slopes_a))
        y_a = y_a * jax.nn.silu(c["a_gate"])
        y_b = flat(mla_attention(c["b_cq"], c["b_ckv"], c["b_kpe"], mla_q_norm[l], mla_w_uq[l],
                                 mla_kv_norm[l], mla_w_ukv[l]))
        y_b = y_b * jax.nn.silu(c["b_gate"])
        g_c = jax.nn.sigmoid(c["c_g"]).reshape(B, S, N_HEADS, 3)
        y_c = flat(nsa_attention(heads(c["c_q"]), c["c_kc"], c["c_vc"], c["c_ks"], c["c_vs"],
                                 c["c_kw"], c["c_vw"], g_c, nsa_pos[l], nsa_w1[l], nsa_b1[l],
                                 nsa_w2[l], nsa_b2[l], slopes_c))
        y_c = y_c * jax.nn.silu(c["c_gate"])
        y_d = flat(stick_breaking_attention(heads(c["d_q"]), heads(c["d_k"]), heads(c["d_v"])))
        y_d = y_d * jax.nn.silu(c["d_gate"])
        merged = 0.0
        for i, y in enumerate((y_a, y_b, y_c, y_d)):
            gate = jax.nn.sigmoid(h @ w_merge[l, i] + b_merge[l, i])
            merged = merged + gate * (y @ w_branch[l, i])
        x = x + merged @ w_out[l]
    return rmsnorm(x, final_norm_g)
```

```python
import functools

import numpy as np
import jax
import jax.numpy as jnp
from jax import lax
from jax.experimental import pallas as pl
from jax.experimental.pallas import tpu as pltpu

F32 = jnp.float32
BF16 = jnp.bfloat16

HEAD_DIM = 64
N_HEADS = 4
BRANCH_WIDTH = N_HEADS * HEAD_DIM
RMS_EPS = 1e-6
DSW_DILATIONS = (1, 4, 16)
DSW_SPAN = 128
MLA_Q_RANK = 192
MLA_KV_RANK = 128
MLA_NOPE_DIM = 64
MLA_ROPE_DIM = 32
ROPE_THETA = 10000.0
NSA_CMP_BLOCK = 32
NSA_CMP_STRIDE = 16
NSA_CMP_HIDDEN = 128
NSA_SEL_BLOCK = 64
NSA_TOP_N = 16
NSA_WINDOW = 512
NSA_KV_DIM = 64
_ALIBI = [2.0 ** (-(i + 1)) for i in range(8)]
SLOPES_A = _ALIBI[0::2]
SLOPES_C = _ALIBI[1::2]

LANE = 128
TILE = 128
VMEM_LIMIT = 56 * 1024 * 1024
NEG = -1e30

C_AQ, C_AK, C_AV, C_AGATE = 0, 256, 512, 768
C_BCQ, C_BCKV, C_BKPE, C_BKPEROT = 1024, 1280, 1408, 1536
C_CG, C_BGATE, C_CQ, C_CSEL, C_CWIN, C_CGATE = 1664, 1792, 2048, 2304, 2432, 2560
C_DQ, C_DK, C_DV, C_DGATE = 2816, 3072, 3328, 3584
C_WIDTH = 3840
C_KVC = C_WIDTH
W_WIDTH = C_WIDTH + 128
ROPE_LANE = 64


def _dot(a, b):
    return jnp.dot(a, b, preferred_element_type=F32)


def _dot_nt(a, b):
    return lax.dot_general(a, b, (((1,), (1,)), ((), ())), preferred_element_type=F32)


def _cparams(sem):
    return pltpu.CompilerParams(dimension_semantics=sem, vmem_limit_bytes=VMEM_LIMIT)


def _const_spec(shape):
    nd = len(shape)
    return pl.BlockSpec(shape, lambda *_: (0,) * nd)


def _inproj_kernel(x_ref, g_ref, w_ref, c_ref, kvc_ref):
    x = x_ref[...]
    ms = jnp.mean(x * x, axis=-1, keepdims=True)
    h = (x * lax.rsqrt(ms + RMS_EPS) * g_ref[...]).astype(BF16)
    cw = 256
    for j in range(C_WIDTH // cw):
        c_ref[:, j * cw:(j + 1) * cw] = _dot(h, w_ref[:, j * cw:(j + 1) * cw]).astype(BF16)
    kvc_ref[...] = _dot(h, w_ref[:, C_KVC:C_KVC + 128]).astype(BF16)


def _inproj(x2, g, w):
    n, d = x2.shape
    tm = 512
    return pl.pallas_call(
        _inproj_kernel,
        grid=(n // tm,),
        in_specs=[pl.BlockSpec((tm, d), lambda i: (i, 0)),
                  _const_spec((1, d)),
                  _const_spec((d, W_WIDTH))],
        out_specs=[pl.BlockSpec((tm, C_WIDTH), lambda i: (i, 0)),
                   pl.BlockSpec((tm, 128), lambda i: (i, 0))],
        out_shape=[jax.ShapeDtypeStruct((n, C_WIDTH), BF16),
                   jax.ShapeDtypeStruct((n, 128), BF16)],
        compiler_params=_cparams(("arbitrary",)),
        name="inproj",
    )(x2, g, w)


def _dsw_kernel(q_ref, k_ref, v_ref, o_ref, lse_ref, *, dil, lq):
    blk = pl.program_id(2)
    row = lax.broadcasted_iota(jnp.int32, (TILE, 2 * TILE), 0)
    col = lax.broadcasted_iota(jnp.int32, (TILE, 2 * TILE), 1)
    for t in range(lq // TILE):
        u0 = blk * lq + t * TILE
        kstart = pl.multiple_of(jnp.maximum(u0 - TILE, 0), TILE)
        d = (u0 - kstart) + row - col
        valid = (d >= 0) & (d <= DSW_SPAN)
        df = d.astype(F32)
        q = q_ref[0, t * TILE:(t + 1) * TILE, :]
        kk = k_ref[0, pl.ds(kstart, 2 * TILE), :]
        vv = v_ref[0, pl.ds(kstart, 2 * TILE), :]
        outs, lses = [], []
        for h in range(N_HEADS):
            hs = slice(h * HEAD_DIM, (h + 1) * HEAD_DIM)
            s = _dot_nt(q[:, hs], kk[:, hs])
            s = jnp.where(valid, s - (SLOPES_A[h] * dil) * df, NEG)
            m = jnp.max(s, axis=-1, keepdims=True)
            p = jnp.exp(s - m)
            l = jnp.sum(p, axis=-1, keepdims=True)
            o = _dot(p.astype(BF16), vv[:, hs]) / l
            outs.append(o)
            lses.append(jnp.broadcast_to(m + jnp.log(l), (TILE, HEAD_DIM)))
        o_ref[0, t * TILE:(t + 1) * TILE, :] = jnp.concatenate(outs, axis=-1).astype(BF16)
        lse_ref[0, t * TILE:(t + 1) * TILE, :] = jnp.concatenate(lses, axis=-1)


def _dsw_attention(c3, dil):
    b, s, _ = c3.shape
    l = s // dil
    lq = min(512, l)
    ncb = C_WIDTH // BRANCH_WIDTH
    cv = c3.reshape(b, l, dil * C_WIDTH)
    qspec = pl.BlockSpec((1, lq, BRANCH_WIDTH), lambda bi, r, i: (bi, i, r * ncb + C_AQ // BRANCH_WIDTH))
    kspec = pl.BlockSpec((1, l, BRANCH_WIDTH), lambda bi, r, i: (bi, 0, r * ncb + C_AK // BRANCH_WIDTH))
    vspec = pl.BlockSpec((1, l, BRANCH_WIDTH), lambda bi, r, i: (bi, 0, r * ncb + C_AV // BRANCH_WIDTH))
    ospec = pl.BlockSpec((1, lq, BRANCH_WIDTH), lambda bi, r, i: (bi, i, r))
    o, lse = pl.pallas_call(
        functools.partial(_dsw_kernel, dil=dil, lq=lq),
        grid=(b, dil, l // lq),
        in_specs=[qspec, kspec, vspec],
        out_specs=[ospec, ospec],
        out_shape=[jax.ShapeDtypeStruct((b, l, dil * BRANCH_WIDTH), BF16),
                   jax.ShapeDtypeStruct((b, l, dil * BRANCH_WIDTH), F32)],
        compiler_params=_cparams(("arbitrary", "arbitrary", "arbitrary")),
        name=f"dsw{dil}",
    )(cv, cv, cv)
    return o.reshape(b * s, BRANCH_WIDTH), lse.reshape(b * s, BRANCH_WIDTH)


def _mla_prep_kernel(cq_ref, ckv_ref, kpe_ref, kper_ref, cos_ref, sin_ref, gq_ref, wl_ref, wr_ref,
                     gkv_ref, wk_ref, wv_ref, q_ref, k_ref, v_ref):
    cos = cos_ref[...]
    sin = sin_ref[...]
    cos4 = jnp.concatenate([cos] * N_HEADS, axis=-1)
    sin4 = jnp.concatenate([sin] * N_HEADS, axis=-1)
    cq = cq_ref[...].astype(F32)
    ms = jnp.sum(cq * cq, axis=-1, keepdims=True) * (1.0 / MLA_Q_RANK)
    qn = (cq * lax.rsqrt(ms + RMS_EPS) * gq_ref[...]).astype(BF16)
    scale = (MLA_NOPE_DIM + MLA_ROPE_DIM) ** -0.5
    q = (_dot(qn, wl_ref[...]) * cos4 + _dot(qn, wr_ref[...]) * sin4) * scale
    q_ref[...] = q.astype(BF16)
    ckv = ckv_ref[...].astype(F32)
    ms = jnp.mean(ckv * ckv, axis=-1, keepdims=True)
    kn = (ckv * lax.rsqrt(ms + RMS_EPS) * gkv_ref[...]).astype(BF16)
    kpe = kpe_ref[...].astype(F32) * cos + kper_ref[...].astype(F32) * sin
    k = _dot(kn, wk_ref[...]) + jnp.concatenate([kpe] * N_HEADS, axis=-1)
    k_ref[...] = k.astype(BF16)
    v_ref[...] = _dot(kn, wv_ref[...]).astype(BF16)


def _mla_prep(c2, s, cos_t, sin_t, gq, wl, wr, gkv, wk, wv):
    n = c2.shape[0]
    tm = 512
    spb = s // tm
    hq = N_HEADS * LANE

    def cspec(off, width):
        return pl.BlockSpec((tm, width), lambda i: (i, off // width))

    tspec = pl.BlockSpec((tm, LANE), lambda i: (i % spb, 0))
    return pl.pallas_call(
        _mla_prep_kernel,
        grid=(n // tm,),
        in_specs=[cspec(C_BCQ, 256), cspec(C_BCKV, 128), cspec(C_BKPE, 128), cspec(C_BKPEROT, 128),
                  tspec, tspec,
                  _const_spec((1, 256)), _const_spec((256, hq)), _const_spec((256, hq)),
                  _const_spec((1, 128)), _const_spec((128, hq)), _const_spec((128, BRANCH_WIDTH))],
        out_specs=[pl.BlockSpec((tm, hq), lambda i: (i, 0)),
                   pl.BlockSpec((tm, hq), lambda i: (i, 0)),
                   pl.BlockSpec((tm, BRANCH_WIDTH), lambda i: (i, 0))],
        out_shape=[jax.ShapeDtypeStruct((n, hq), BF16),
                   jax.ShapeDtypeStruct((n, hq), BF16),
                   jax.ShapeDtypeStruct((n, BRANCH_WIDTH), BF16)],
        compiler_params=_cparams(("arbitrary",)),
        name="mla_prep",
    )(c2, c2, c2, c2, cos_t, sin_t, gq, wl, wr, gkv, wk, wv)


def _mla_flash_kernel(q_ref, k_ref, v_ref, o_ref):
    i = pl.program_id(1)
    row = lax.broadcasted_iota(jnp.int32, (TILE, TILE), 0)
    col = lax.broadcasted_iota(jnp.int32, (TILE, TILE), 1)
    causal = col <= row
    outs = []
    for h in range(N_HEADS):
        qh = q_ref[0, :, h * LANE:(h + 1) * LANE]

        def step(j, carry, masked, h=h, qh=qh):
            m, l, acc = carry
            ks = pl.multiple_of(j * TILE, TILE)
            kt = k_ref[0, pl.ds(ks, TILE), h * LANE:(h + 1) * LANE]
            vt = v_ref[0, pl.ds(ks, TILE), h * HEAD_DIM:(h + 1) * HEAD_DIM]
            s = _dot_nt(qh, kt)
            if masked:
                s = jnp.where(causal, s, NEG)
            m_new = jnp.maximum(m, jnp.max(s, axis=-1, keepdims=True))
            alpha = jnp.exp(m - m_new)
            p = jnp.exp(s - m_new)
            l = alpha * l + jnp.sum(p, axis=-1, keepdims=True)
            acc = alpha * acc + _dot(p.astype(BF16), vt)
            return m_new, l, acc

        init = (jnp.full((TILE, 1), NEG, F32), jnp.zeros((TILE, 1), F32), jnp.zeros((TILE, HEAD_DIM), F32))
        carry = lax.fori_loop(0, i, functools.partial(step, masked=False), init)
        m, l, acc = step(i, carry, True)
        outs.append(acc / l)
    o_ref[0] = jnp.concatenate(outs, axis=-1).astype(BF16)


def _mla_flash(q3, k3, v3):
    b, s, hq = q3.shape
    return pl.pallas_call(
        _mla_flash_kernel,
        grid=(b, s // TILE),
        in_specs=[pl.BlockSpec((1, TILE, hq), lambda bi, i: (bi, i, 0)),
                  pl.BlockSpec((1, s, hq), lambda bi, i: (bi, 0, 0)),
                  pl.BlockSpec((1, s, BRANCH_WIDTH), lambda bi, i: (bi, 0, 0))],
        out_specs=pl.BlockSpec((1, TILE, BRANCH_WIDTH), lambda bi, i: (bi, i, 0)),
        out_shape=jax.ShapeDtypeStruct((b, s, BRANCH_WIDTH), BF16),
        compiler_params=_cparams(("arbitrary", "arbitrary")),
        name="mla_flash",
    )(q3, k3, v3)


def _nsa_cmp_kernel(ch_ref, pt_ref, pb_ref, wt_ref, wb_ref, b1_ref, w2_ref, b2_ref, o_ref):
    ch = ch_ref[0]
    top = _dot(ch, wt_ref[...])
    bot = _dot(ch, wb_ref[...])
    const = (_dot(pt_ref[...], wt_ref[...]) + _dot(pb_ref[...], wb_ref[...]))[0:1, :] + b1_ref[...]
    n = ch.shape[0]
    hid = top + pltpu.roll(bot, n - 1, 0) + const
    hid = jax.nn.gelu(hid)
    o_ref[0] = (_dot(hid.astype(BF16), w2_ref[...]) + b2_ref[...]).astype(BF16)


def _nsa_compress(kvc3, pos_t, pos_b, wt, wb, b1, w2, b2):
    b, s, _ = kvc3.shape
    nch = s // NSA_CMP_STRIDE
    cw = NSA_CMP_STRIDE * 128
    ch = kvc3.reshape(b, nch, cw)
    return pl.pallas_call(
        _nsa_cmp_kernel,
        grid=(b,),
        in_specs=[pl.BlockSpec((1, nch, cw), lambda bi: (bi, 0, 0)),
                  _const_spec((8, cw)), _const_spec((8, cw)),
                  _const_spec((cw, 2 * NSA_CMP_HIDDEN)), _const_spec((cw, 2 * NSA_CMP_HIDDEN)),
                  _const_spec((1, 2 * NSA_CMP_HIDDEN)),
                  _const_spec((2 * NSA_CMP_HIDDEN, 2 * NSA_KV_DIM)), _const_spec((1, 2 * NSA_KV_DIM))],
        out_specs=pl.BlockSpec((1, nch, 2 * NSA_KV_DIM), lambda bi: (bi, 0, 0)),
        out_shape=jax.ShapeDtypeStruct((b, nch, 2 * NSA_KV_DIM), BF16),
        compiler_params=_cparams(("arbitrary",)),
        name="nsa_compress",
    )(ch, pos_t, pos_b, wt, wb, b1, w2, b2)


def _split_dot(x, w, terms=3):
    out = None
    rem = x
    for _ in range(terms):
        part = rem.astype(BF16)
        rem = rem - part.astype(F32)
        y = _dot(part, w)
        out = y if out is None else out + y
    return out


def _nsa_kernel(q_ref, g_ref, cmp_ref, sel_ref, win_ref, o_ref, *, ncmp):
    i = pl.program_id(1)
    t0 = i * TILE
    hr = N_HEADS * TILE
    q = q_ref[0]
    q4 = jnp.concatenate([q[:, h * HEAD_DIM:(h + 1) * HEAD_DIM] for h in range(N_HEADS)], axis=0)
    rowh = lax.broadcasted_iota(jnp.int32, (hr, 1), 0)
    hidx = rowh >> 7
    r = rowh & (TILE - 1)
    slope = jnp.where(hidx == 0, SLOPES_C[0],
                      jnp.where(hidx == 1, SLOPES_C[1], jnp.where(hidx == 2, SLOPES_C[2], SLOPES_C[3]))).astype(F32)

    def masked_softmax(s, mask):
        sb = jnp.where(mask, s, NEG)
        m = jnp.max(sb, axis=-1, keepdims=True)
        e = jnp.where(mask, jnp.exp(sb - m), 0.0)
        return e / jnp.maximum(jnp.sum(e, axis=-1, keepdims=True), 1e-30)

    kc = cmp_ref[0, :, 0:NSA_KV_DIM]
    vc = cmp_ref[0, :, NSA_KV_DIM:2 * NSA_KV_DIM]
    ncol = lax.broadcasted_iota(jnp.int32, (1, ncmp), 1)
    dist_c = (t0 + r) - (ncol * NSA_CMP_STRIDE + (NSA_CMP_BLOCK - 1))
    s = _dot_nt(q4, kc) - slope * dist_c.astype(F32)
    p_cmp = masked_softmax(s, dist_c >= 0)
    o_cmp = _dot(p_cmp.astype(BF16), vc)
    psum = p_cmp[0:TILE] + p_cmp[TILE:2 * TILE] + p_cmp[2 * TILE:3 * TILE] + p_cmp[3 * TILE:4 * TILE]
    nrow = lax.broadcasted_iota(jnp.int32, (ncmp, TILE), 0)
    jcol = lax.broadcasted_iota(jnp.int32, (ncmp, TILE), 1)
    delta = jcol * (NSA_SEL_BLOCK // NSA_CMP_STRIDE) - nrow
    selmap_t = jnp.where((delta == 0) | (delta == 4), 1.0,
                         jnp.where((delta > 0) & (delta < 4), 2.0, 0.0)).astype(BF16)
    p_sel = _split_dot(psum, selmap_t)

    blk = lax.broadcasted_iota(jnp.int32, (TILE, TILE), 1)
    tq = t0 + lax.broadcasted_iota(jnp.int32, (TILE, TILE), 0)
    cur = tq >> 6
    forced = (blk == 0) | (blk == cur) | (blk == cur - 1)
    cand = jnp.where((blk <= cur) & jnp.logical_not(forced), p_sel, -1.0)
    picked = jnp.where(forced, 1.0, 0.0)
    for _ in range(NSA_TOP_N - 3):
        m = jnp.max(cand, axis=-1, keepdims=True)
        idx = jnp.min(jnp.where(cand == m, blk, TILE), axis=-1, keepdims=True)
        hit = (blk == idx) & (m >= 0.0)
        picked = jnp.where(hit, 1.0, picked)
        cand = jnp.where(hit, -1.0, cand)
    selb = picked.astype(BF16)

    brow = lax.broadcasted_iota(jnp.int32, (TILE, TILE), 0)
    ccol = lax.broadcasted_iota(jnp.int32, (TILE, TILE), 1)
    blk_of = brow - (ccol >> 6)
    col4 = lax.broadcasted_iota(jnp.int32, (hr, TILE), 1)
    rc = r - col4

    def sel_step(j, carry, diag):
        m, l, acc = carry
        ks = pl.multiple_of(j * TILE, TILE)
        expand = jnp.where(blk_of == 2 * j, 1.0, 0.0).astype(BF16)
        mk = _dot(selb, expand)
        mk4 = jnp.concatenate([mk] * N_HEADS, axis=0) > 0.5
        dist = (t0 - ks) + rc
        if diag:
            mk4 = mk4 & (dist >= 0)
        kt = sel_ref[0, pl.ds(ks, TILE), 0:NSA_KV_DIM]
        vt = sel_ref[0, pl.ds(ks, TILE), NSA_KV_DIM:2 * NSA_KV_DIM]
        sb = jnp.where(mk4, _dot_nt(q4, kt) - slope * dist.astype(F32), NEG)
        m_new = jnp.maximum(m, jnp.max(sb, axis=-1, keepdims=True))
        alpha = jnp.exp(m - m_new)
        p = jnp.where(mk4, jnp.exp(sb - m_new), 0.0)
        l = alpha * l + jnp.sum(p, axis=-1, keepdims=True)
        acc = alpha * acc + _dot(p.astype(BF16), vt)
        return m_new, l, acc

    init = (jnp.full((hr, 1), NEG, F32), jnp.zeros((hr, 1), F32), jnp.zeros((hr, HEAD_DIM), F32))
    carry = lax.fori_loop(0, i, functools.partial(sel_step, diag=False), init)
    _, l, acc = sel_step(i, carry, True)
    o_slc = acc / jnp.maximum(l, 1e-30)

    wlen = NSA_WINDOW + TILE
    kstart = pl.multiple_of(jnp.maximum(t0 - NSA_WINDOW, 0), TILE)
    kw = win_ref[0, pl.ds(kstart, wlen), 0:NSA_KV_DIM]
    vw = win_ref[0, pl.ds(kstart, wlen), NSA_KV_DIM:2 * NSA_KV_DIM]
    colw = lax.broadcasted_iota(jnp.int32, (hr, wlen), 1)
    dist_w = (t0 - kstart) + r - colw
    s = _dot_nt(q4, kw) - slope * dist_w.astype(F32)
    p_win = masked_softmax(s, (dist_w >= 0) & (dist_w < NSA_WINDOW))
    o_win = _dot(p_win.astype(BF16), vw)

    g = jax.nn.sigmoid(g_ref[0].astype(F32))

    def gate_col(kk):
        return jnp.concatenate([g[:, 3 * h + kk:3 * h + kk + 1] for h in range(N_HEADS)], axis=0)

    out4 = gate_col(0) * o_cmp + gate_col(1) * o_slc + gate_col(2) * o_win
    o_ref[0] = jnp.concatenate([out4[h * TILE:(h + 1) * TILE] for h in range(N_HEADS)], axis=-1).astype(BF16)


def _nsa_attention(c3, cmp3):
    b, s, _ = c3.shape
    ncmp = cmp3.shape[1]
    return pl.pallas_call(
        functools.partial(_nsa_kernel, ncmp=ncmp),
        grid=(b, s // TILE),
        in_specs=[pl.BlockSpec((1, TILE, BRANCH_WIDTH), lambda bi, i: (bi, i, C_CQ // BRANCH_WIDTH)),
                  pl.BlockSpec((1, TILE, 128), lambda bi, i: (bi, i, C_CG // 128)),
                  pl.BlockSpec((1, ncmp, 128), lambda bi, i: (bi, 0, 0)),
                  pl.BlockSpec((1, s, 128), lambda bi, i: (bi, 0, C_CSEL // 128)),
                  pl.BlockSpec((1, s, 128), lambda bi, i: (bi, 0, C_CWIN // 128))],
        out_specs=pl.BlockSpec((1, TILE, BRANCH_WIDTH), lambda bi, i: (bi, i, 0)),
        out_shape=jax.ShapeDtypeStruct((b, s, BRANCH_WIDTH), BF16),
        compiler_params=_cparams(("arbitrary", "arbitrary")),
        name="nsa_attn",
    )(c3, c3, cmp3, c3, c3)


def _sb_kernel(q_ref, k_ref, v_ref, o_ref):
    i = pl.program_id(1)
    row = lax.broadcasted_iota(jnp.int32, (TILE, TILE), 0)
    col = lax.broadcasted_iota(jnp.int32, (TILE, TILE), 1)
    strict = col < row
    urow = lax.broadcasted_iota(jnp.int32, (2 * TILE, 2 * TILE), 0) & (TILE - 1)
    ucol = lax.broadcasted_iota(jnp.int32, (2 * TILE, 2 * TILE), 1)
    umat = jnp.where((ucol >= TILE) | (urow > ucol), 1.0, 0.0).astype(BF16)
    outs = []
    for h in range(N_HEADS):
        hs = slice(h * HEAD_DIM, (h + 1) * HEAD_DIM)
        qh = q_ref[0, :, hs]

        def step(j, carry, diag, hs=hs, qh=qh):
            run, acc = carry
            ks = pl.multiple_of(j * TILE, TILE)
            kt = k_ref[0, pl.ds(ks, TILE), hs]
            vt = v_ref[0, pl.ds(ks, TILE), hs]
            z = _dot_nt(qh, kt)
            lp = jnp.log1p(jnp.exp(-jnp.abs(z)))
            ls = jnp.minimum(z, 0.0) - lp
            l1m = -jnp.maximum(z, 0.0) - lp
            if diag:
                l1m = jnp.where(strict, l1m, 0.0)
            hi = l1m.astype(BF16)
            lo = (l1m - hi.astype(F32)).astype(BF16)
            cs = _dot(jnp.concatenate([hi, lo], axis=-1), umat)
            a = jnp.exp(ls + run + cs[:, 0:TILE])
            if diag:
                a = jnp.where(strict, a, 0.0)
            acc = acc + _dot(a.astype(BF16), vt)
            return run + cs[:, TILE:2 * TILE], acc

        carry = step(i, (jnp.zeros((TILE, TILE), F32), jnp.zeros((TILE, HEAD_DIM), F32)), True)
        _, acc = lax.fori_loop(0, i, lambda jj, cr: step(i - 1 - jj, cr, False), carry)
        outs.append(acc)
    o_ref[0] = jnp.concatenate(outs, axis=-1).astype(BF16)


def _sb_attention(c3):
    b, s, _ = c3.shape
    return pl.pallas_call(
        _sb_kernel,
        grid=(b, s // TILE),
        in_specs=[pl.BlockSpec((1, TILE, BRANCH_WIDTH), lambda bi, i: (bi, i, C_DQ // BRANCH_WIDTH)),
                  pl.BlockSpec((1, s, BRANCH_WIDTH), lambda bi, i: (bi, 0, C_DK // BRANCH_WIDTH)),
                  pl.BlockSpec((1, s, BRANCH_WIDTH), lambda bi, i: (bi, 0, C_DV // BRANCH_WIDTH))],
        out_specs=pl.BlockSpec((1, TILE, BRANCH_WIDTH), lambda bi, i: (bi, i, 0)),
        out_shape=jax.ShapeDtypeStruct((b, s, BRANCH_WIDTH), BF16),
        compiler_params=_cparams(("arbitrary", "arbitrary")),
        name="stickbreak",
    )(c3, c3, c3)


def _merge_kernel(x_ref, ng_ref, o1_ref, o2_ref, o3_ref, l1_ref, l2_ref, l3_ref, ga_ref,
                  yb_ref, gb_ref, yc_ref, gc_ref, yd_ref, gd_ref,
                  wm_ref, bm_ref, wb_ref, wo_ref, fg_ref, out_ref, *, final):
    x = x_ref[...]
    ms = jnp.mean(x * x, axis=-1, keepdims=True)
    h = (x * lax.rsqrt(ms + RMS_EPS) * ng_ref[...]).astype(BF16)
    l1, l2, l3 = l1_ref[...], l2_ref[...], l3_ref[...]
    lm = jnp.maximum(jnp.maximum(l1, l2), l3)
    e1, e2, e3 = jnp.exp(l1 - lm), jnp.exp(l2 - lm), jnp.exp(l3 - lm)
    ya = (e1 * o1_ref[...].astype(F32) + e2 * o2_ref[...].astype(F32) + e3 * o3_ref[...].astype(F32)) / (e1 + e2 + e3)
    ys = (ya, yb_ref[...].astype(F32), yc_ref[...].astype(F32), yd_ref[...].astype(F32))
    gs = (ga_ref, gb_ref, gc_ref, gd_ref)
    merged = None
    for i in range(4):
        y = (ys[i] * jax.nn.silu(gs[i][...].astype(F32))).astype(BF16)
        gate = jax.nn.sigmoid(_dot(h, wm_ref[i]) + bm_ref[i])
        term = gate * _dot(y, wb_ref[i])
        merged = term if merged is None else merged + term
    xn = x + _dot(merged.astype(BF16), wo_ref[...])
    if final:
        ms = jnp.mean(xn * xn, axis=-1, keepdims=True)
        xn = xn * lax.rsqrt(ms + RMS_EPS) * fg_ref[...]
    out_ref[...] = xn


def _merge(x2, ng, a_outs, a_lses, c2, yb, yc, yd, wm, bm, wb, wo, fg, final):
    n, d = x2.shape
    tm = 256
    bw = BRANCH_WIDTH

    def rows(width):
        return pl.BlockSpec((tm, width), lambda i: (i, 0))

    def cspec(off):
        return pl.BlockSpec((tm, bw), lambda i: (i, off // bw))

    return pl.pallas_call(
        functools.partial(_merge_kernel, final=final),
        grid=(n // tm,),
        in_specs=[rows(d), _const_spec((1, d)),
                  rows(bw), rows(bw), rows(bw), rows(bw), rows(bw), rows(bw), cspec(C_AGATE),
                  rows(bw), cspec(C_BGATE), rows(bw), cspec(C_CGATE), rows(bw), cspec(C_DGATE),
                  _const_spec((4, d, d)), _const_spec((4, 1, d)), _const_spec((4, bw, d)),
                  _const_spec((d, d)), _const_spec((1, d))],
        out_specs=rows(d),
        out_shape=jax.ShapeDtypeStruct((n, d), F32),
        compiler_params=_cparams(("arbitrary",)),
        name="merge",
    )(x2, ng, *a_outs, *a_lses, c2, yb, c2, yc, c2, yd, c2, wm, bm, wb, wo, fg)


def _split_w_in(w):
    widths = (256, 256, 256, 256, MLA_Q_RANK, MLA_KV_RANK, MLA_ROPE_DIM, 256,
              256, 64, 64, 64, 64, 64, 64, 3 * N_HEADS, 256, 256, 256, 256, 256)
    names = ("a_q", "a_k", "a_v", "a_gate", "b_cq", "b_ckv", "b_kpe", "b_gate",
             "c_q", "c_kc", "c_vc", "c_ks", "c_vs", "c_kw", "c_vw", "c_g", "c_gate",
             "d_q", "d_k", "d_v", "d_gate")
    out, off = {}, 0
    for nm, wd in zip(names, widths):
        out[nm] = w[:, off:off + wd]
        off += wd
    return out


def _rot_cols(w):
    half = w.shape[1] // 2
    return jnp.concatenate([-w[:, half:], w[:, :half]], axis=1)


def _prep_w_in(w):
    p = _split_w_in(w)
    d = w.shape[0]
    qs = HEAD_DIM ** -0.5

    def z(n):
        return jnp.zeros((d, n), w.dtype)

    kpe_blk = jnp.concatenate([z(ROPE_LANE), p["b_kpe"], z(LANE - ROPE_LANE - MLA_ROPE_DIM)], axis=1)
    kper_blk = jnp.concatenate([z(ROPE_LANE), _rot_cols(p["b_kpe"]), z(LANE - ROPE_LANE - MLA_ROPE_DIM)], axis=1)
    cols = [p["a_q"] * qs, p["a_k"], p["a_v"], p["a_gate"],
            p["b_cq"], z(256 - MLA_Q_RANK), p["b_ckv"], kpe_blk, kper_blk,
            p["c_g"], z(128 - 3 * N_HEADS), p["b_gate"],
            p["c_q"] * qs, p["c_ks"], p["c_vs"], p["c_kw"], p["c_vw"], p["c_gate"],
            p["d_q"] * qs, p["d_k"], p["d_v"], p["d_gate"],
            p["c_kc"], p["c_vc"]]
    out = jnp.concatenate(cols, axis=1)
    assert out.shape[1] == W_WIDTH
    return out.astype(BF16)


def _prep_mla(w_uq, w_ukv, gq):
    qd = MLA_NOPE_DIM + MLA_ROPE_DIM
    pad = LANE - qd
    wl, wr, wk, wv = [], [], [], []
    for h in range(N_HEADS):
        nope = w_uq[:, h * qd:h * qd + MLA_NOPE_DIM]
        rp = w_uq[:, h * qd + MLA_NOPE_DIM:(h + 1) * qd]
        zq = jnp.zeros((MLA_Q_RANK, pad), w_uq.dtype)
        wl += [nope, rp, zq]
        wr += [jnp.zeros_like(nope), _rot_cols(rp), zq]
        kn = w_ukv[:, h * 128:h * 128 + MLA_NOPE_DIM]
        wk += [kn, jnp.zeros((MLA_KV_RANK, LANE - MLA_NOPE_DIM), w_ukv.dtype)]
        wv.append(w_ukv[:, h * 128 + MLA_NOPE_DIM:(h + 1) * 128])
    rpad = ((0, 256 - MLA_Q_RANK), (0, 0))
    wl = jnp.pad(jnp.concatenate(wl, axis=1), rpad).astype(BF16)
    wr = jnp.pad(jnp.concatenate(wr, axis=1), rpad).astype(BF16)
    gq = jnp.pad(gq, (0, 256 - MLA_Q_RANK)).reshape(1, 256)
    return wl, wr, jnp.concatenate(wk, axis=1).astype(BF16), jnp.concatenate(wv, axis=1).astype(BF16), gq


def _rope_tables(s):
    half = MLA_ROPE_DIM // 2
    inv = ROPE_THETA ** (-jnp.arange(half, dtype=F32) / half)
    ang = jnp.arange(s, dtype=F32)[:, None] * inv[None, :]
    cos, sin = jnp.cos(ang), jnp.sin(ang)
    ones = jnp.ones((s, ROPE_LANE), F32)
    zl = jnp.zeros((s, ROPE_LANE), F32)
    zr = jnp.zeros((s, LANE - ROPE_LANE - MLA_ROPE_DIM), F32)
    return (jnp.concatenate([ones, cos, cos, zr], axis=1), jnp.concatenate([zl, sin, sin, zr], axis=1))


def _prep_nsa(pos, w1, b1, w2, b2):
    half = NSA_CMP_BLOCK // 2
    eye = jnp.eye(2, dtype=w1.dtype)
    w1r = w1.reshape(2, NSA_CMP_BLOCK, NSA_KV_DIM, NSA_CMP_HIDDEN)
    posr = pos

    def comb_w(part):
        return jnp.einsum("ktdh,kj->tkdjh", part, eye).reshape(half * 2 * NSA_KV_DIM, 2 * NSA_CMP_HIDDEN)

    def comb_p(part):
        flat = jnp.transpose(part, (1, 0, 2)).reshape(1, half * 2 * NSA_KV_DIM)
        return jnp.pad(flat, ((0, 7), (0, 0))).astype(BF16)

    wt, wb = comb_w(w1r[:, :half]).astype(BF16), comb_w(w1r[:, half:]).astype(BF16)
    pt, pb = comb_p(posr[:, :half]), comb_p(posr[:, half:])
    b1c = b1.reshape(1, 2 * NSA_CMP_HIDDEN)
    zero = jnp.zeros((NSA_CMP_HIDDEN, NSA_KV_DIM), w2.dtype)
    w2c = jnp.concatenate([jnp.concatenate([w2[0], zero], axis=1),
                           jnp.concatenate([zero, w2[1]], axis=1)], axis=0).astype(BF16)
    b2c = b2.reshape(1, 2 * NSA_KV_DIM)
    return pt, pb, wt, wb, b1c, w2c, b2c


def kernel(x, norm_g, w_in, mla_q_norm, mla_w_uq, mla_kv_norm, mla_w_ukv, nsa_pos, nsa_w1, nsa_b1,
           nsa_w2, nsa_b2, w_branch, w_merge, b_merge, w_out, final_norm_g):
    b, s, d = x.shape
    depth = norm_g.shape[0]
    n = b * s
    assert s % (TILE * max(DSW_DILATIONS)) == 0 and s >= NSA_WINDOW + TILE
    cos_t, sin_t = _rope_tables(s)
    x2 = x.reshape(n, d)
    for layer in range(depth):
        c2, kvc = _inproj(x2, norm_g[layer].reshape(1, d), _prep_w_in(w_in[layer]))
        c3 = c2.reshape(b, s, C_WIDTH)
        a_res = [_dsw_attention(c3, dil) for dil in DSW_DILATIONS]
        wl, wr, wk, wv, gq = _prep_mla(mla_w_uq[layer], mla_w_ukv[layer], mla_q_norm[layer])
        qb, kb, vb = _mla_prep(c2, s, cos_t, sin_t, gq, wl, wr, mla_kv_norm[layer].reshape(1, MLA_KV_RANK), wk, wv)
        hq = N_HEADS * LANE
        yb = _mla_flash(qb.reshape(b, s, hq), kb.reshape(b, s, hq), vb.reshape(b, s, BRANCH_WIDTH))
        cmp3 = _nsa_compress(kvc.reshape(b, s, 128),
                             *_prep_nsa(nsa_pos[layer], nsa_w1[layer], nsa_b1[layer], nsa_w2[layer], nsa_b2[layer]))
        yc = _nsa_attention(c3, cmp3)
        yd = _sb_attention(c3)
        x2 = _merge(x2, norm_g[layer].reshape(1, d), [r[0] for r in a_res], [r[1] for r in a_res], c2,
                    yb.reshape(n, BRANCH_WIDTH), yc.reshape(n, BRANCH_WIDTH), yd.reshape(n, BRANCH_WIDTH),
                    w_merge[layer].astype(BF16), b_merge[layer].reshape(4, 1, d),
                    w_branch[layer].astype(BF16), w_out[layer].astype(BF16),
                    final_norm_g.reshape(1, d), final=(layer == depth - 1))
    return x2.reshape(b, s, d)
```

```python
import functools

import numpy as np
import jax
import jax.numpy as jnp
from jax import lax
from jax.experimental import pallas as pl
from jax.experimental.pallas import tpu as pltpu

F32 = jnp.float32
BF16 = jnp.bfloat16

HEAD_DIM = 64
N_HEADS = 4
BRANCH_WIDTH = N_HEADS * HEAD_DIM
RMS_EPS = 1e-6
DSW_DILATIONS = (1, 4, 16)
DSW_SPAN = 128
MLA_Q_RANK = 192
MLA_KV_RANK = 128
MLA_NOPE_DIM = 64
MLA_ROPE_DIM = 32
ROPE_THETA = 10000.0
NSA_CMP_BLOCK = 32
NSA_CMP_STRIDE = 16
NSA_CMP_HIDDEN = 128
NSA_SEL_BLOCK = 64
NSA_TOP_N = 16
NSA_WINDOW = 512
NSA_KV_DIM = 64
_ALIBI = [2.0 ** (-(i + 1)) for i in range(8)]
SLOPES_A = _ALIBI[0::2]
SLOPES_C = _ALIBI[1::2]

LANE = 128
TILE = 128
VMEM_LIMIT = 56 * 1024 * 1024
NEG = -1e30
MLA_TQ, MLA_TK = 256, 512
SB_TQ = 256
SB_UNDERFLOW = -104.0

C_AQ, C_AK, C_AV, C_AGATE = 0, 256, 512, 768
C_BCQ, C_BCKV, C_BKPE, C_BKPEROT = 1024, 1280, 1408, 1536
C_CG, C_BGATE, C_CQ, C_CSEL, C_CWIN, C_CGATE = 1664, 1792, 2048, 2304, 2432, 2560
C_DQ, C_DK, C_DV, C_DGATE = 2816, 3072, 3328, 3584
C_WIDTH = 3840
C_KVC = C_WIDTH
W_WIDTH = C_WIDTH + 128
ROPE_LANE = 64


def _dot(a, b):
    return jnp.dot(a, b, preferred_element_type=F32)


def _dot_nt(a, b):
    return lax.dot_general(a, b, (((1,), (1,)), ((), ())), preferred_element_type=F32)


def _cparams(sem):
    return pltpu.CompilerParams(dimension_semantics=sem, vmem_limit_bytes=VMEM_LIMIT)


def _const_spec(shape):
    nd = len(shape)
    return pl.BlockSpec(shape, lambda *_: (0,) * nd)


def _inproj_kernel(x_ref, g_ref, w_ref, c_ref, kvc_ref):
    x = x_ref[...]
    ms = jnp.mean(x * x, axis=-1, keepdims=True)
    h = (x * lax.rsqrt(ms + RMS_EPS) * g_ref[...]).astype(BF16)
    cw = 256
    for j in range(C_WIDTH // cw):
        c_ref[:, j * cw:(j + 1) * cw] = _dot(h, w_ref[:, j * cw:(j + 1) * cw]).astype(BF16)
    kvc_ref[...] = _dot(h, w_ref[:, C_KVC:C_KVC + 128]).astype(BF16)


def _inproj(x2, g, w):
    n, d = x2.shape
    tm = 512
    return pl.pallas_call(
        _inproj_kernel,
        grid=(n // tm,),
        in_specs=[pl.BlockSpec((tm, d), lambda i: (i, 0)),
                  _const_spec((1, d)),
                  _const_spec((d, W_WIDTH))],
        out_specs=[pl.BlockSpec((tm, C_WIDTH), lambda i: (i, 0)),
                   pl.BlockSpec((tm, 128), lambda i: (i, 0))],
        out_shape=[jax.ShapeDtypeStruct((n, C_WIDTH), BF16),
                   jax.ShapeDtypeStruct((n, 128), BF16)],
        compiler_params=_cparams(("arbitrary",)),
        name="inproj",
    )(x2, g, w)


def _dsw_kernel(q_ref, k_ref, v_ref, o_ref, lse_ref, *, dil, lq):
    blk = pl.program_id(2)
    row = lax.broadcasted_iota(jnp.int32, (TILE, 2 * TILE), 0)
    col = lax.broadcasted_iota(jnp.int32, (TILE, 2 * TILE), 1)
    for t in range(lq // TILE):
        u0 = blk * lq + t * TILE
        kstart = pl.multiple_of(jnp.maximum(u0 - TILE, 0), TILE)
        d = (u0 - kstart) + row - col
        valid = (d >= 0) & (d <= DSW_SPAN)
        df = d.astype(F32)
        q = q_ref[0, t * TILE:(t + 1) * TILE, :]
        kk = k_ref[0, pl.ds(kstart, 2 * TILE), :]
        vv = v_ref[0, pl.ds(kstart, 2 * TILE), :]
        outs, lses = [], []
        for h in range(N_HEADS):
            hs = slice(h * HEAD_DIM, (h + 1) * HEAD_DIM)
            s = _dot_nt(q[:, hs], kk[:, hs])
            s = jnp.where(valid, s - (SLOPES_A[h] * dil) * df, NEG)
            m = jnp.max(s, axis=-1, keepdims=True)
            p = jnp.exp(s - m)
            l = jnp.sum(p, axis=-1, keepdims=True)
            o = _dot(p.astype(BF16), vv[:, hs]) / l
            outs.append(o)
            lses.append(jnp.broadcast_to(m + jnp.log(l), (TILE, HEAD_DIM)))
        o_ref[0, t * TILE:(t + 1) * TILE, :] = jnp.concatenate(outs, axis=-1).astype(BF16)
        lse_ref[0, t * TILE:(t + 1) * TILE, :] = jnp.concatenate(lses, axis=-1)


def _dsw_attention(c3, dil):
    b, s, _ = c3.shape
    l = s // dil
    lq = min(512, l)
    ncb = C_WIDTH // BRANCH_WIDTH
    cv = c3.reshape(b, l, dil * C_WIDTH)
    qspec = pl.BlockSpec((1, lq, BRANCH_WIDTH), lambda bi, r, i: (bi, i, r * ncb + C_AQ // BRANCH_WIDTH))
    kspec = pl.BlockSpec((1, l, BRANCH_WIDTH), lambda bi, r, i: (bi, 0, r * ncb + C_AK // BRANCH_WIDTH))
    vspec = pl.BlockSpec((1, l, BRANCH_WIDTH), lambda bi, r, i: (bi, 0, r * ncb + C_AV // BRANCH_WIDTH))
    ospec = pl.BlockSpec((1, lq, BRANCH_WIDTH), lambda bi, r, i: (bi, i, r))
    o, lse = pl.pallas_call(
        functools.partial(_dsw_kernel, dil=dil, lq=lq),
        grid=(b, dil, l // lq),
        in_specs=[qspec, kspec, vspec],
        out_specs=[ospec, ospec],
        out_shape=[jax.ShapeDtypeStruct((b, l, dil * BRANCH_WIDTH), BF16),
                   jax.ShapeDtypeStruct((b, l, dil * BRANCH_WIDTH), F32)],
        compiler_params=_cparams(("arbitrary", "arbitrary", "arbitrary")),
        name=f"dsw{dil}",
    )(cv, cv, cv)
    return o.reshape(b * s, BRANCH_WIDTH), lse.reshape(b * s, BRANCH_WIDTH)


def _mla_prep_kernel(cq_ref, ckv_ref, kpe_ref, kper_ref, cos_ref, sin_ref, gq_ref, wl_ref, wr_ref,
                     gkv_ref, wk_ref, wv_ref, q_ref, k_ref, v_ref):
    cos = cos_ref[...]
    sin = sin_ref[...]
    cos4 = jnp.concatenate([cos] * N_HEADS, axis=-1)
    sin4 = jnp.concatenate([sin] * N_HEADS, axis=-1)
    cq = cq_ref[...].astype(F32)
    ms = jnp.sum(cq * cq, axis=-1, keepdims=True) * (1.0 / MLA_Q_RANK)
    qn = (cq * lax.rsqrt(ms + RMS_EPS) * gq_ref[...]).astype(BF16)
    scale = (MLA_NOPE_DIM + MLA_ROPE_DIM) ** -0.5
    q = (_dot(qn, wl_ref[...]) * cos4 + _dot(qn, wr_ref[...]) * sin4) * scale
    q_ref[...] = q.astype(BF16)
    ckv = ckv_ref[...].astype(F32)
    ms = jnp.mean(ckv * ckv, axis=-1, keepdims=True)
    kn = (ckv * lax.rsqrt(ms + RMS_EPS) * gkv_ref[...]).astype(BF16)
    kpe = kpe_ref[...].astype(F32) * cos + kper_ref[...].astype(F32) * sin
    k = _dot(kn, wk_ref[...]) + jnp.concatenate([kpe] * N_HEADS, axis=-1)
    k_ref[...] = k.astype(BF16)
    v_ref[...] = _dot(kn, wv_ref[...]).astype(BF16)


def _mla_prep(c2, s, cos_t, sin_t, gq, wl, wr, gkv, wk, wv):
    n = c2.shape[0]
    tm = 512
    spb = s // tm
    hq = N_HEADS * LANE

    def cspec(off, width):
        return pl.BlockSpec((tm, width), lambda i: (i, off // width))

    tspec = pl.BlockSpec((tm, LANE), lambda i: (i % spb, 0))
    return pl.pallas_call(
        _mla_prep_kernel,
        grid=(n // tm,),
        in_specs=[cspec(C_BCQ, 256), cspec(C_BCKV, 128), cspec(C_BKPE, 128), cspec(C_BKPEROT, 128),
                  tspec, tspec,
                  _const_spec((1, 256)), _const_spec((256, hq)), _const_spec((256, hq)),
                  _const_spec((1, 128)), _const_spec((128, hq)), _const_spec((128, BRANCH_WIDTH))],
        out_specs=[pl.BlockSpec((tm, hq), lambda i: (i, 0)),
                   pl.BlockSpec((tm, hq), lambda i: (i, 0)),
                   pl.BlockSpec((tm, BRANCH_WIDTH), lambda i: (i, 0))],
        out_shape=[jax.ShapeDtypeStruct((n, hq), BF16),
                   jax.ShapeDtypeStruct((n, hq), BF16),
                   jax.ShapeDtypeStruct((n, BRANCH_WIDTH), BF16)],
        compiler_params=_cparams(("arbitrary",)),
        name="mla_prep",
    )(c2, c2, c2, c2, cos_t, sin_t, gq, wl, wr, gkv, wk, wv)


def _mla_flash_kernel(q_ref, k_ref, v_ref, o_ref):
    i = pl.program_id(1)
    t0 = i * MLA_TQ
    row = lax.broadcasted_iota(jnp.int32, (MLA_TQ, MLA_TK), 0)
    col = lax.broadcasted_iota(jnp.int32, (MLA_TQ, MLA_TK), 1)

    def step(j, carry, masked):
        ks = pl.multiple_of(j * MLA_TK, MLA_TK)
        if masked:
            keep = (col - row) <= (t0 - ks)
        new = []
        for h in range(N_HEADS):
            m, l, acc = carry[h]
            qh = q_ref[0, :, h * LANE:(h + 1) * LANE]
            kt = k_ref[0, pl.ds(ks, MLA_TK), h * LANE:(h + 1) * LANE]
            vt = v_ref[0, pl.ds(ks, MLA_TK), h * HEAD_DIM:(h + 1) * HEAD_DIM]
            s = _dot_nt(qh, kt)
            if masked:
                s = jnp.where(keep, s, NEG)
            m_new = jnp.maximum(m, jnp.max(s, axis=-1, keepdims=True))
            alpha = jnp.exp(m - m_new)
            p = jnp.exp(s - m_new)
            l = alpha * l + jnp.sum(p, axis=-1, keepdims=True)
            acc = alpha * acc + _dot(p.astype(BF16), vt)
            new.append((m_new, l, acc))
        return tuple(new)

    init = tuple((jnp.full((MLA_TQ, 1), NEG, F32), jnp.zeros((MLA_TQ, 1), F32),
                  jnp.zeros((MLA_TQ, HEAD_DIM), F32)) for _ in range(N_HEADS))
    nfull = t0 // MLA_TK
    carry = lax.fori_loop(0, nfull, functools.partial(step, masked=False), init)
    carry = step(nfull, carry, True)
    o_ref[0] = jnp.concatenate([acc / l for (_, l, acc) in carry], axis=-1).astype(BF16)


def _mla_flash(q3, k3, v3):
    b, s, hq = q3.shape
    return pl.pallas_call(
        _mla_flash_kernel,
        grid=(b, s // MLA_TQ),
        in_specs=[pl.BlockSpec((1, MLA_TQ, hq), lambda bi, i: (bi, i, 0)),
                  pl.BlockSpec((1, s, hq), lambda bi, i: (bi, 0, 0)),
                  pl.BlockSpec((1, s, BRANCH_WIDTH), lambda bi, i: (bi, 0, 0))],
        out_specs=pl.BlockSpec((1, MLA_TQ, BRANCH_WIDTH), lambda bi, i: (bi, i, 0)),
        out_shape=jax.ShapeDtypeStruct((b, s, BRANCH_WIDTH), BF16),
        compiler_params=_cparams(("arbitrary", "arbitrary")),
        name="mla_flash",
    )(q3, k3, v3)


def _nsa_cmp_kernel(ch_ref, pt_ref, pb_ref, wt_ref, wb_ref, b1_ref, w2_ref, b2_ref, o_ref):
    ch = ch_ref[0]
    top = _dot(ch, wt_ref[...])
    bot = _dot(ch, wb_ref[...])
    const = (_dot(pt_ref[...], wt_ref[...]) + _dot(pb_ref[...], wb_ref[...]))[0:1, :] + b1_ref[...]
    n = ch.shape[0]
    hid = top + pltpu.roll(bot, n - 1, 0) + const
    hid = jax.nn.gelu(hid)
    o_ref[0] = (_dot(hid.astype(BF16), w2_ref[...]) + b2_ref[...]).astype(BF16)


def _nsa_compress(kvc3, pos_t, pos_b, wt, wb, b1, w2, b2):
    b, s, _ = kvc3.shape
    nch = s // NSA_CMP_STRIDE
    cw = NSA_CMP_STRIDE * 128
    ch = kvc3.reshape(b, nch, cw)
    return pl.pallas_call(
        _nsa_cmp_kernel,
        grid=(b,),
        in_specs=[pl.BlockSpec((1, nch, cw), lambda bi: (bi, 0, 0)),
                  _const_spec((8, cw)), _const_spec((8, cw)),
                  _const_spec((cw, 2 * NSA_CMP_HIDDEN)), _const_spec((cw, 2 * NSA_CMP_HIDDEN)),
                  _const_spec((1, 2 * NSA_CMP_HIDDEN)),
                  _const_spec((2 * NSA_CMP_HIDDEN, 2 * NSA_KV_DIM)), _const_spec((1, 2 * NSA_KV_DIM))],
        out_specs=pl.BlockSpec((1, nch, 2 * NSA_KV_DIM), lambda bi: (bi, 0, 0)),
        out_shape=jax.ShapeDtypeStruct((b, nch, 2 * NSA_KV_DIM), BF16),
        compiler_params=_cparams(("arbitrary",)),
        name="nsa_compress",
    )(ch, pos_t, pos_b, wt, wb, b1, w2, b2)


def _split_dot(x, w, terms=3):
    out = None
    rem = x
    for _ in range(terms):
        part = rem.astype(BF16)
        rem = rem - part.astype(F32)
        y = _dot(part, w)
        out = y if out is None else out + y
    return out


def _nsa_kernel(q_ref, g_ref, cmp_ref, sel_ref, win_ref, o_ref, *, ncmp):
    i = pl.program_id(1)
    t0 = i * TILE
    hr = N_HEADS * TILE
    q = q_ref[0]
    q4 = jnp.concatenate([q[:, h * HEAD_DIM:(h + 1) * HEAD_DIM] for h in range(N_HEADS)], axis=0)
    rowh = lax.broadcasted_iota(jnp.int32, (hr, 1), 0)
    hidx = rowh >> 7
    r = rowh & (TILE - 1)
    slope = jnp.where(hidx == 0, SLOPES_C[0],
                      jnp.where(hidx == 1, SLOPES_C[1], jnp.where(hidx == 2, SLOPES_C[2], SLOPES_C[3]))).astype(F32)

    def masked_softmax(s, mask):
        sb = jnp.where(mask, s, NEG)
        m = jnp.max(sb, axis=-1, keepdims=True)
        e = jnp.where(mask, jnp.exp(sb - m), 0.0)
        return e / jnp.maximum(jnp.sum(e, axis=-1, keepdims=True), 1e-30)

    kc = cmp_ref[0, :, 0:NSA_KV_DIM]
    vc = cmp_ref[0, :, NSA_KV_DIM:2 * NSA_KV_DIM]
    ncol = lax.broadcasted_iota(jnp.int32, (1, ncmp), 1)
    dist_c = (t0 + r) - (ncol * NSA_CMP_STRIDE + (NSA_CMP_BLOCK - 1))
    s = _dot_nt(q4, kc) - slope * dist_c.astype(F32)
    p_cmp = masked_softmax(s, dist_c >= 0)
    o_cmp = _dot(p_cmp.astype(BF16), vc)
    psum = p_cmp[0:TILE] + p_cmp[TILE:2 * TILE] + p_cmp[2 * TILE:3 * TILE] + p_cmp[3 * TILE:4 * TILE]
    nrow = lax.broadcasted_iota(jnp.int32, (ncmp, TILE), 0)
    jcol = lax.broadcasted_iota(jnp.int32, (ncmp, TILE), 1)
    delta = jcol * (NSA_SEL_BLOCK // NSA_CMP_STRIDE) - nrow
    selmap_t = jnp.where((delta == 0) | (delta == 4), 1.0,
                         jnp.where((delta > 0) & (delta < 4), 2.0, 0.0)).astype(BF16)
    p_sel = _split_dot(psum, selmap_t)

    blk = lax.broadcasted_iota(jnp.int32, (TILE, TILE), 1)
    tq = t0 + lax.broadcasted_iota(jnp.int32, (TILE, TILE), 0)
    cur = tq >> 6
    forced = (blk == 0) | (blk == cur) | (blk == cur - 1)
    cand = jnp.where((blk <= cur) & jnp.logical_not(forced), p_sel, -1.0)
    picked = jnp.where(forced, 1.0, 0.0)
    for _ in range(NSA_TOP_N - 3):
        m = jnp.max(cand, axis=-1, keepdims=True)
        idx = jnp.min(jnp.where(cand == m, blk, TILE), axis=-1, keepdims=True)
        hit = (blk == idx) & (m >= 0.0)
        picked = jnp.where(hit, 1.0, picked)
        cand = jnp.where(hit, -1.0, cand)
    selb = picked.astype(BF16)

    brow = lax.broadcasted_iota(jnp.int32, (TILE, TILE), 0)
    ccol = lax.broadcasted_iota(jnp.int32, (TILE, TILE), 1)
    blk_of = brow - (ccol >> 6)
    col4 = lax.broadcasted_iota(jnp.int32, (hr, TILE), 1)
    rc = r - col4

    def sel_step(j, carry, diag):
        m, l, acc = carry
        ks = pl.multiple_of(j * TILE, TILE)
        expand = jnp.where(blk_of == 2 * j, 1.0, 0.0).astype(BF16)
        mk = _dot(selb, expand)
        mk4 = jnp.concatenate([mk] * N_HEADS, axis=0) > 0.5
        dist = (t0 - ks) + rc
        if diag:
            mk4 = mk4 & (dist >= 0)
        kt = sel_ref[0, pl.ds(ks, TILE), 0:NSA_KV_DIM]
        vt = sel_ref[0, pl.ds(ks, TILE), NSA_KV_DIM:2 * NSA_KV_DIM]
        sb = jnp.where(mk4, _dot_nt(q4, kt) - slope * dist.astype(F32), NEG)
        m_new = jnp.maximum(m, jnp.max(sb, axis=-1, keepdims=True))
        alpha = jnp.exp(m - m_new)
        p = jnp.where(mk4, jnp.exp(sb - m_new), 0.0)
        l = alpha * l + jnp.sum(p, axis=-1, keepdims=True)
        acc = alpha * acc + _dot(p.astype(BF16), vt)
        return m_new, l, acc

    init = (jnp.full((hr, 1), NEG, F32), jnp.zeros((hr, 1), F32), jnp.zeros((hr, HEAD_DIM), F32))
    carry = lax.fori_loop(0, i, functools.partial(sel_step, diag=False), init)
    _, l, acc = sel_step(i, carry, True)
    o_slc = acc / jnp.maximum(l, 1e-30)

    wlen = NSA_WINDOW + TILE
    kstart = pl.multiple_of(jnp.maximum(t0 - NSA_WINDOW, 0), TILE)
    kw = win_ref[0, pl.ds(kstart, wlen), 0:NSA_KV_DIM]
    vw = win_ref[0, pl.ds(kstart, wlen), NSA_KV_DIM:2 * NSA_KV_DIM]
    colw = lax.broadcasted_iota(jnp.int32, (hr, wlen), 1)
    dist_w = (t0 - kstart) + r - colw
    s = _dot_nt(q4, kw) - slope * dist_w.astype(F32)
    p_win = masked_softmax(s, (dist_w >= 0) & (dist_w < NSA_WINDOW))
    o_win = _dot(p_win.astype(BF16), vw)

    g = jax.nn.sigmoid(g_ref[0].astype(F32))

    def gate_col(kk):
        return jnp.concatenate([g[:, 3 * h + kk:3 * h + kk + 1] for h in range(N_HEADS)], axis=0)

    out4 = gate_col(0) * o_cmp + gate_col(1) * o_slc + gate_col(2) * o_win
    o_ref[0] = jnp.concatenate([out4[h * TILE:(h + 1) * TILE] for h in range(N_HEADS)], axis=-1).astype(BF16)


def _nsa_attention(c3, cmp3):
    b, s, _ = c3.shape
    ncmp = cmp3.shape[1]
    return pl.pallas_call(
        functools.partial(_nsa_kernel, ncmp=ncmp),
        grid=(b, s // TILE),
        in_specs=[pl.BlockSpec((1, TILE, BRANCH_WIDTH), lambda bi, i: (bi, i, C_CQ // BRANCH_WIDTH)),
                  pl.BlockSpec((1, TILE, 128), lambda bi, i: (bi, i, C_CG // 128)),
                  pl.BlockSpec((1, ncmp, 128), lambda bi, i: (bi, 0, 0)),
                  pl.BlockSpec((1, s, 128), lambda bi, i: (bi, 0, C_CSEL // 128)),
                  pl.BlockSpec((1, s, 128), lambda bi, i: (bi, 0, C_CWIN // 128))],
        out_specs=pl.BlockSpec((1, TILE, BRANCH_WIDTH), lambda bi, i: (bi, i, 0)),
        out_shape=jax.ShapeDtypeStruct((b, s, BRANCH_WIDTH), BF16),
        compiler_params=_cparams(("arbitrary", "arbitrary")),
        name="nsa_attn",
    )(c3, c3, cmp3, c3, c3)


def _sb_kernel(q_ref, k_ref, v_ref, o_ref, run_ref, acc_ref):
    i = pl.program_id(1)
    t0 = i * SB_TQ
    row = lax.broadcasted_iota(jnp.int32, (SB_TQ, TILE), 0)
    col = lax.broadcasted_iota(jnp.int32, (SB_TQ, TILE), 1)
    urow = lax.broadcasted_iota(jnp.int32, (2 * TILE, 2 * TILE), 0) & (TILE - 1)
    ucol = lax.broadcasted_iota(jnp.int32, (2 * TILE, 2 * TILE), 1)
    umat = jnp.where((ucol >= TILE) | (urow > ucol), 1.0, 0.0).astype(BF16)
    run_ref[...] = jnp.zeros_like(run_ref)
    acc_ref[...] = jnp.zeros_like(acc_ref)

    def tile(j, masked):
        ks = pl.multiple_of(j * TILE, TILE)
        if masked:
            strict = (col - row) < (t0 - ks)
        for h in range(N_HEADS):
            hs = slice(h * HEAD_DIM, (h + 1) * HEAD_DIM)
            z = _dot_nt(q_ref[0, :, hs], k_ref[0, pl.ds(ks, TILE), hs])
            lp = jnp.log1p(jnp.exp(-jnp.abs(z)))
            ls = jnp.minimum(z, 0.0) - lp
            l1m = -jnp.maximum(z, 0.0) - lp
            if masked:
                l1m = jnp.where(strict, l1m, 0.0)
            hi = l1m.astype(BF16)
            lo = (l1m - hi.astype(F32)).astype(BF16)
            cs = _dot(jnp.concatenate([hi, lo], axis=-1), umat)
            run = run_ref[h]
            a = jnp.exp(ls + run + cs[:, 0:TILE])
            if masked:
                a = jnp.where(strict, a, 0.0)
            acc_ref[h] += _dot(a.astype(BF16), v_ref[0, pl.ds(ks, TILE), hs])
            run_ref[h] = run + cs[:, TILE:2 * TILE]

    def run_max():
        return jnp.max(jnp.maximum(jnp.maximum(run_ref[0], run_ref[1]), jnp.maximum(run_ref[2], run_ref[3])))

    j_top = t0 // TILE + SB_TQ // TILE - 1
    for dj in range(SB_TQ // TILE):
        tile(j_top - dj, True)

    def cond(c):
        return (c[0] >= 0) & (c[1] > SB_UNDERFLOW)

    def body(c):
        tile(c[0], False)
        return c[0] - 1, run_max()

    lax.while_loop(cond, body, (j_top - SB_TQ // TILE, run_max()))
    o_ref[0] = jnp.concatenate([acc_ref[h] for h in range(N_HEADS)], axis=-1).astype(BF16)


def _sb_attention(c3):
    b, s, _ = c3.shape
    return pl.pallas_call(
        _sb_kernel,
        grid=(b, s // SB_TQ),
        in_specs=[pl.BlockSpec((1, SB_TQ, BRANCH_WIDTH), lambda bi, i: (bi, i, C_DQ // BRANCH_WIDTH)),
                  pl.BlockSpec((1, s, BRANCH_WIDTH), lambda bi, i: (bi, 0, C_DK // BRANCH_WIDTH)),
                  pl.BlockSpec((1, s, BRANCH_WIDTH), lambda bi, i: (bi, 0, C_DV // BRANCH_WIDTH))],
        out_specs=pl.BlockSpec((1, SB_TQ, BRANCH_WIDTH), lambda bi, i: (bi, i, 0)),
        out_shape=jax.ShapeDtypeStruct((b, s, BRANCH_WIDTH), BF16),
        scratch_shapes=[pltpu.VMEM((N_HEADS, SB_TQ, TILE), F32), pltpu.VMEM((N_HEADS, SB_TQ, HEAD_DIM), F32)],
        compiler_params=_cparams(("arbitrary", "arbitrary")),
        name="stickbreak",
    )(c3, c3, c3)


def _merge_kernel(x_ref, ng_ref, o1_ref, o2_ref, o3_ref, l1_ref, l2_ref, l3_ref, ga_ref,
                  yb_ref, gb_ref, yc_ref, gc_ref, yd_ref, gd_ref,
                  wm_ref, bm_ref, wb_ref, wo_ref, fg_ref, out_ref, *, final):
    x = x_ref[...]
    ms = jnp.mean(x * x, axis=-1, keepdims=True)
    h = (x * lax.rsqrt(ms + RMS_EPS) * ng_ref[...]).astype(BF16)
    l1, l2, l3 = l1_ref[...], l2_ref[...], l3_ref[...]
    lm = jnp.maximum(jnp.maximum(l1, l2), l3)
    e1, e2, e3 = jnp.exp(l1 - lm), jnp.exp(l2 - lm), jnp.exp(l3 - lm)
    ya = (e1 * o1_ref[...].astype(F32) + e2 * o2_ref[...].astype(F32) + e3 * o3_ref[...].astype(F32)) / (e1 + e2 + e3)
    ys = (ya, yb_ref[...].astype(F32), yc_ref[...].astype(F32), yd_ref[...].astype(F32))
    gs = (ga_ref, gb_ref, gc_ref, gd_ref)
    merged = None
    for i in range(4):
        y = (ys[i] * jax.nn.silu(gs[i][...].astype(F32))).astype(BF16)
        gate = jax.nn.sigmoid(_dot(h, wm_ref[i]) + bm_ref[i])
        term = gate * _dot(y, wb_ref[i])
        merged = term if merged is None else merged + term
    xn = x + _dot(merged.astype(BF16), wo_ref[...])
    if final:
        ms = jnp.mean(xn * xn, axis=-1, keepdims=True)
        xn = xn * lax.rsqrt(ms + RMS_EPS) * fg_ref[...]
    out_ref[...] = xn


def _merge(x2, ng, a_outs, a_lses, c2, yb, yc, yd, wm, bm, wb, wo, fg, final):
    n, d = x2.shape
    tm = 256
    bw = BRANCH_WIDTH

    def rows(width):
        return pl.BlockSpec((tm, width), lambda i: (i, 0))

    def cspec(off):
        return pl.BlockSpec((tm, bw), lambda i: (i, off // bw))

    return pl.pallas_call(
        functools.partial(_merge_kernel, final=final),
        grid=(n // tm,),
        in_specs=[rows(d), _const_spec((1, d)),
                  rows(bw), rows(bw), rows(bw), rows(bw), rows(bw), rows(bw), cspec(C_AGATE),
                  rows(bw), cspec(C_BGATE), rows(bw), cspec(C_CGATE), rows(bw), cspec(C_DGATE),
                  _const_spec((4, d, d)), _const_spec((4, 1, d)), _const_spec((4, bw, d)),
                  _const_spec((d, d)), _const_spec((1, d))],
        out_specs=rows(d),
        out_shape=jax.ShapeDtypeStruct((n, d), F32),
        compiler_params=_cparams(("arbitrary",)),
        name="merge",
    )(x2, ng, *a_outs, *a_lses, c2, yb, c2, yc, c2, yd, c2, wm, bm, wb, wo, fg)


def _split_w_in(w):
    widths = (256, 256, 256, 256, MLA_Q_RANK, MLA_KV_RANK, MLA_ROPE_DIM, 256,
              256, 64, 64, 64, 64, 64, 64, 3 * N_HEADS, 256, 256, 256, 256, 256)
    names = ("a_q", "a_k", "a_v", "a_gate", "b_cq", "b_ckv", "b_kpe", "b_gate",
             "c_q", "c_kc", "c_vc", "c_ks", "c_vs", "c_kw", "c_vw", "c_g", "c_gate",
             "d_q", "d_k", "d_v", "d_gate")
    out, off = {}, 0
    for nm, wd in zip(names, widths):
        out[nm] = w[:, off:off + wd]
        off += wd
    return out


def _rot_cols(w):
    half = w.shape[1] // 2
    return jnp.concatenate([-w[:, half:], w[:, :half]], axis=1)


def _prep_w_in(w):
    p = _split_w_in(w)
    d = w.shape[0]
    qs = HEAD_DIM ** -0.5

    def z(n):
        return jnp.zeros((d, n), w.dtype)

    kpe_blk = jnp.concatenate([z(ROPE_LANE), p["b_kpe"], z(LANE - ROPE_LANE - MLA_ROPE_DIM)], axis=1)
    kper_blk = jnp.concatenate([z(ROPE_LANE), _rot_cols(p["b_kpe"]), z(LANE - ROPE_LANE - MLA_ROPE_DIM)], axis=1)
    cols = [p["a_q"] * qs, p["a_k"], p["a_v"], p["a_gate"],
            p["b_cq"], z(256 - MLA_Q_RANK), p["b_ckv"], kpe_blk, kper_blk,
            p["c_g"], z(128 - 3 * N_HEADS), p["b_gate"],
            p["c_q"] * qs, p["c_ks"], p["c_vs"], p["c_kw"], p["c_vw"], p["c_gate"],
            p["d_q"] * qs, p["d_k"], p["d_v"], p["d_gate"],
            p["c_kc"], p["c_vc"]]
    out = jnp.concatenate(cols, axis=1)
    assert out.shape[1] == W_WIDTH
    return out.astype(BF16)


def _prep_mla(w_uq, w_ukv, gq):
    qd = MLA_NOPE_DIM + MLA_ROPE_DIM
    pad = LANE - qd
    wl, wr, wk, wv = [], [], [], []
    for h in range(N_HEADS):
        nope = w_uq[:, h * qd:h * qd + MLA_NOPE_DIM]
        rp = w_uq[:, h * qd + MLA_NOPE_DIM:(h + 1) * qd]
        zq = jnp.zeros((MLA_Q_RANK, pad), w_uq.dtype)
        wl += [nope, rp, zq]
        wr += [jnp.zeros_like(nope), _rot_cols(rp), zq]
        kn = w_ukv[:, h * 128:h * 128 + MLA_NOPE_DIM]
        wk += [kn, jnp.zeros((MLA_KV_RANK, LANE - MLA_NOPE_DIM), w_ukv.dtype)]
        wv.append(w_ukv[:, h * 128 + MLA_NOPE_DIM:(h + 1) * 128])
    rpad = ((0, 256 - MLA_Q_RANK), (0, 0))
    wl = jnp.pad(jnp.concatenate(wl, axis=1), rpad).astype(BF16)
    wr = jnp.pad(jnp.concatenate(wr, axis=1), rpad).astype(BF16)
    gq = jnp.pad(gq, (0, 256 - MLA_Q_RANK)).reshape(1, 256)
    return wl, wr, jnp.concatenate(wk, axis=1).astype(BF16), jnp.concatenate(wv, axis=1).astype(BF16), gq


def _rope_tables(s):
    half = MLA_ROPE_DIM // 2
    inv = ROPE_THETA ** (-jnp.arange(half, dtype=F32) / half)
    ang = jnp.arange(s, dtype=F32)[:, None] * inv[None, :]
    cos, sin = jnp.cos(ang), jnp.sin(ang)
    ones = jnp.ones((s, ROPE_LANE), F32)
    zl = jnp.zeros((s, ROPE_LANE), F32)
    zr = jnp.zeros((s, LANE - ROPE_LANE - MLA_ROPE_DIM), F32)
    return (jnp.concatenate([ones, cos, cos, zr], axis=1), jnp.concatenate([zl, sin, sin, zr], axis=1))


def _prep_nsa(pos, w1, b1, w2, b2):
    half = NSA_CMP_BLOCK // 2
    eye = jnp.eye(2, dtype=w1.dtype)
    w1r = w1.reshape(2, NSA_CMP_BLOCK, NSA_KV_DIM, NSA_CMP_HIDDEN)
    posr = pos

    def comb_w(part):
        return jnp.einsum("ktdh,kj->tkdjh", part, eye).reshape(half * 2 * NSA_KV_DIM, 2 * NSA_CMP_HIDDEN)

    def comb_p(part):
        flat = jnp.transpose(part, (1, 0, 2)).reshape(1, half * 2 * NSA_KV_DIM)
        return jnp.pad(flat, ((0, 7), (0, 0))).astype(BF16)

    wt, wb = comb_w(w1r[:, :half]).astype(BF16), comb_w(w1r[:, half:]).astype(BF16)
    pt, pb = comb_p(posr[:, :half]), comb_p(posr[:, half:])
    b1c = b1.reshape(1, 2 * NSA_CMP_HIDDEN)
    zero = jnp.zeros((NSA_CMP_HIDDEN, NSA_KV_DIM), w2.dtype)
    w2c = jnp.concatenate([jnp.concatenate([w2[0], zero], axis=1),
                           jnp.concatenate([zero, w2[1]], axis=1)], axis=0).astype(BF16)
    b2c = b2.reshape(1, 2 * NSA_KV_DIM)
    return pt, pb, wt, wb, b1c, w2c, b2c


def kernel(x, norm_g, w_in, mla_q_norm, mla_w_uq, mla_kv_norm, mla_w_ukv, nsa_pos, nsa_w1, nsa_b1,
           nsa_w2, nsa_b2, w_branch, w_merge, b_merge, w_out, final_norm_g):
    b, s, d = x.shape
    depth = norm_g.shape[0]
    n = b * s
    assert s % (TILE * max(DSW_DILATIONS)) == 0 and s >= NSA_WINDOW + TILE
    cos_t, sin_t = _rope_tables(s)
    x2 = x.reshape(n, d)
    for layer in range(depth):
        c2, kvc = _inproj(x2, norm_g[layer].reshape(1, d), _prep_w_in(w_in[layer]))
        c3 = c2.reshape(b, s, C_WIDTH)
        a_res = [_dsw_attention(c3, dil) for dil in DSW_DILATIONS]
        wl, wr, wk, wv, gq = _prep_mla(mla_w_uq[layer], mla_w_ukv[layer], mla_q_norm[layer])
        qb, kb, vb = _mla_prep(c2, s, cos_t, sin_t, gq, wl, wr, mla_kv_norm[layer].reshape(1, MLA_KV_RANK), wk, wv)
        hq = N_HEADS * LANE
        yb = _mla_flash(qb.reshape(b, s, hq), kb.reshape(b, s, hq), vb.reshape(b, s, BRANCH_WIDTH))
        cmp3 = _nsa_compress(kvc.reshape(b, s, 128),
                             *_prep_nsa(nsa_pos[layer], nsa_w1[layer], nsa_b1[layer], nsa_w2[layer], nsa_b2[layer]))
        yc = _nsa_attention(c3, cmp3)
        yd = _sb_attention(c3)
        x2 = _merge(x2, norm_g[layer].reshape(1, d), [r[0] for r in a_res], [r[1] for r in a_res], c2,
                    yb.reshape(n, BRANCH_WIDTH), yc.reshape(n, BRANCH_WIDTH), yd.reshape(n, BRANCH_WIDTH),
                    w_merge[layer].astype(BF16), b_merge[layer].reshape(4, 1, d),
                    w_branch[layer].astype(BF16), w_out[layer].astype(BF16),
                    final_norm_g.reshape(1, d), final=(layer == depth - 1))
    return x2.reshape(b, s, d)
```

```python
import functools

import numpy as np
import jax
import jax.numpy as jnp
from jax import lax
from jax.experimental import pallas as pl
from jax.experimental.pallas import tpu as pltpu

F32 = jnp.float32
BF16 = jnp.bfloat16

HEAD_DIM = 64
N_HEADS = 4
BRANCH_WIDTH = N_HEADS * HEAD_DIM
RMS_EPS = 1e-6
DSW_DILATIONS = (1, 4, 16)
DSW_SPAN = 128
MLA_Q_RANK = 192
MLA_KV_RANK = 128
MLA_NOPE_DIM = 64
MLA_ROPE_DIM = 32
ROPE_THETA = 10000.0
NSA_CMP_BLOCK = 32
NSA_CMP_STRIDE = 16
NSA_CMP_HIDDEN = 128
NSA_SEL_BLOCK = 64
NSA_TOP_N = 16
NSA_WINDOW = 512
NSA_KV_DIM = 64
_ALIBI = [2.0 ** (-(i + 1)) for i in range(8)]
SLOPES_A = _ALIBI[0::2]
SLOPES_C = _ALIBI[1::2]

LANE = 128
TILE = 128
VMEM_LIMIT = 56 * 1024 * 1024
NEG = -1e30
MLA_TQ, MLA_TK = 256, 512
SB_TQ = 256
SB_UNDERFLOW = -104.0

A_WIDTH = 3 * 256
C_AGATE = 0
C_BCQ, C_BCKV, C_BKPE, C_BKPEROT = 256, 512, 640, 768
C_CG, C_BGATE, C_CQ, C_CSEL, C_CWIN, C_CGATE = 896, 1024, 1280, 1536, 1664, 1792
C_DQ, C_DK, C_DV, C_DGATE = 2048, 2304, 2560, 2816
C_WIDTH = 3072
W_C = A_WIDTH
W_KVC = A_WIDTH + C_WIDTH
W_WIDTH = W_KVC + 128
NSA_TK = 512
ROPE_LANE = 64


def _dot(a, b):
    return jnp.dot(a, b, preferred_element_type=F32)


def _dot_nt(a, b):
    return lax.dot_general(a, b, (((1,), (1,)), ((), ())), preferred_element_type=F32)


def _cparams(sem):
    return pltpu.CompilerParams(dimension_semantics=sem, vmem_limit_bytes=VMEM_LIMIT)


def _const_spec(shape):
    nd = len(shape)
    return pl.BlockSpec(shape, lambda *_: (0,) * nd)


def _inproj_kernel(x_ref, g_ref, w_ref, a_ref, c_ref, kvc_ref):
    x = x_ref[...]
    ms = jnp.mean(x * x, axis=-1, keepdims=True)
    h = (x * lax.rsqrt(ms + RMS_EPS) * g_ref[...]).astype(BF16)
    cw = 256
    for j in range(A_WIDTH // cw):
        a_ref[:, j * cw:(j + 1) * cw] = _dot(h, w_ref[:, j * cw:(j + 1) * cw]).astype(BF16)
    for j in range(C_WIDTH // cw):
        c_ref[:, j * cw:(j + 1) * cw] = _dot(h, w_ref[:, W_C + j * cw:W_C + (j + 1) * cw]).astype(BF16)
    kvc_ref[...] = _dot(h, w_ref[:, W_KVC:W_KVC + 128]).astype(BF16)


def _inproj(x2, g, w):
    n, d = x2.shape
    tm = 512
    return pl.pallas_call(
        _inproj_kernel,
        grid=(n // tm,),
        in_specs=[pl.BlockSpec((tm, d), lambda i: (i, 0)),
                  _const_spec((1, d)),
                  _const_spec((d, W_WIDTH))],
        out_specs=[pl.BlockSpec((tm, A_WIDTH), lambda i: (i, 0)),
                   pl.BlockSpec((tm, C_WIDTH), lambda i: (i, 0)),
                   pl.BlockSpec((tm, 128), lambda i: (i, 0))],
        out_shape=[jax.ShapeDtypeStruct((n, A_WIDTH), BF16),
                   jax.ShapeDtypeStruct((n, C_WIDTH), BF16),
                   jax.ShapeDtypeStruct((n, 128), BF16)],
        compiler_params=_cparams(("arbitrary",)),
        name="inproj",
    )(x2, g, w)


def _dsw_kernel(q_ref, k_ref, v_ref, o_ref, lse_ref, *, dil, lq):
    blk = pl.program_id(2)
    row = lax.broadcasted_iota(jnp.int32, (TILE, 2 * TILE), 0)
    col = lax.broadcasted_iota(jnp.int32, (TILE, 2 * TILE), 1)
    for t in range(lq // TILE):
        u0 = blk * lq + t * TILE
        kstart = pl.multiple_of(jnp.maximum(u0 - TILE, 0), TILE)
        d = (u0 - kstart) + row - col
        valid = (d >= 0) & (d <= DSW_SPAN)
        df = d.astype(F32)
        q = q_ref[0, t * TILE:(t + 1) * TILE, :]
        kk = k_ref[0, pl.ds(kstart, 2 * TILE), :]
        vv = v_ref[0, pl.ds(kstart, 2 * TILE), :]
        outs, lses = [], []
        for h in range(N_HEADS):
            hs = slice(h * HEAD_DIM, (h + 1) * HEAD_DIM)
            s = _dot_nt(q[:, hs], kk[:, hs])
            s = jnp.where(valid, s - (SLOPES_A[h] * dil) * df, NEG)
            m = jnp.max(s, axis=-1, keepdims=True)
            p = jnp.exp(s - m)
            l = jnp.sum(p, axis=-1, keepdims=True)
            o = _dot(p.astype(BF16), vv[:, hs]) / l
            outs.append(o)
            lses.append(jnp.broadcast_to(m + jnp.log(l), (TILE, HEAD_DIM)))
        o_ref[0, t * TILE:(t + 1) * TILE, :] = jnp.concatenate(outs, axis=-1).astype(BF16)
        lse_ref[0, t * TILE:(t + 1) * TILE, :] = jnp.concatenate(lses, axis=-1)


def _dsw_attention(a3, dil):
    b, s, _ = a3.shape
    l = s // dil
    lq = min(512, l)
    ncb = A_WIDTH // BRANCH_WIDTH
    cv = a3.reshape(b, l, dil * A_WIDTH)
    qspec = pl.BlockSpec((1, lq, BRANCH_WIDTH), lambda bi, r, i: (bi, i, r * ncb))
    kspec = pl.BlockSpec((1, l, BRANCH_WIDTH), lambda bi, r, i: (bi, 0, r * ncb + 1))
    vspec = pl.BlockSpec((1, l, BRANCH_WIDTH), lambda bi, r, i: (bi, 0, r * ncb + 2))
    ospec = pl.BlockSpec((1, lq, BRANCH_WIDTH), lambda bi, r, i: (bi, i, r))
    o, lse = pl.pallas_call(
        functools.partial(_dsw_kernel, dil=dil, lq=lq),
        grid=(b, dil, l // lq),
        in_specs=[qspec, kspec, vspec],
        out_specs=[ospec, ospec],
        out_shape=[jax.ShapeDtypeStruct((b, l, dil * BRANCH_WIDTH), BF16),
                   jax.ShapeDtypeStruct((b, l, dil * BRANCH_WIDTH), F32)],
        compiler_params=_cparams(("arbitrary", "arbitrary", "arbitrary")),
        name=f"dsw{dil}",
    )(cv, cv, cv)
    return o.reshape(b * s, BRANCH_WIDTH), lse.reshape(b * s, BRANCH_WIDTH)


def _mla_prep_kernel(cq_ref, ckv_ref, kpe_ref, kper_ref, cos_ref, sin_ref, gq_ref, wl_ref, wr_ref,
                     gkv_ref, wk_ref, wv_ref, q_ref, k_ref, v_ref):
    cos = cos_ref[...]
    sin = sin_ref[...]
    cos4 = jnp.concatenate([cos] * N_HEADS, axis=-1)
    sin4 = jnp.concatenate([sin] * N_HEADS, axis=-1)
    cq = cq_ref[...].astype(F32)
    ms = jnp.sum(cq * cq, axis=-1, keepdims=True) * (1.0 / MLA_Q_RANK)
    qn = (cq * lax.rsqrt(ms + RMS_EPS) * gq_ref[...]).astype(BF16)
    scale = (MLA_NOPE_DIM + MLA_ROPE_DIM) ** -0.5
    q = (_dot(qn, wl_ref[...]) * cos4 + _dot(qn, wr_ref[...]) * sin4) * scale
    q_ref[...] = q.astype(BF16)
    ckv = ckv_ref[...].astype(F32)
    ms = jnp.mean(ckv * ckv, axis=-1, keepdims=True)
    kn = (ckv * lax.rsqrt(ms + RMS_EPS) * gkv_ref[...]).astype(BF16)
    kpe = kpe_ref[...].astype(F32) * cos + kper_ref[...].astype(F32) * sin
    k = _dot(kn, wk_ref[...]) + jnp.concatenate([kpe] * N_HEADS, axis=-1)
    k_ref[...] = k.astype(BF16)
    v_ref[...] = _dot(kn, wv_ref[...]).astype(BF16)


def _mla_prep(c2, s, cos_t, sin_t, gq, wl, wr, gkv, wk, wv):
    n = c2.shape[0]
    tm = 512
    spb = s // tm
    hq = N_HEADS * LANE

    def cspec(off, width):
        return pl.BlockSpec((tm, width), lambda i: (i, off // width))

    tspec = pl.BlockSpec((tm, LANE), lambda i: (i % spb, 0))
    return pl.pallas_call(
        _mla_prep_kernel,
        grid=(n // tm,),
        in_specs=[cspec(C_BCQ, 256), cspec(C_BCKV, 128), cspec(C_BKPE, 128), cspec(C_BKPEROT, 128),
                  tspec, tspec,
                  _const_spec((1, 256)), _const_spec((256, hq)), _const_spec((256, hq)),
                  _const_spec((1, 128)), _const_spec((128, hq)), _const_spec((128, BRANCH_WIDTH))],
        out_specs=[pl.BlockSpec((tm, hq), lambda i: (i, 0)),
                   pl.BlockSpec((tm, hq), lambda i: (i, 0)),
                   pl.BlockSpec((tm, BRANCH_WIDTH), lambda i: (i, 0))],
        out_shape=[jax.ShapeDtypeStruct((n, hq), BF16),
                   jax.ShapeDtypeStruct((n, hq), BF16),
                   jax.ShapeDtypeStruct((n, BRANCH_WIDTH), BF16)],
        compiler_params=_cparams(("arbitrary",)),
        name="mla_prep",
    )(c2, c2, c2, c2, cos_t, sin_t, gq, wl, wr, gkv, wk, wv)


def _mla_flash_kernel(q_ref, k_ref, v_ref, o_ref):
    i = pl.program_id(1)
    t0 = i * MLA_TQ
    row = lax.broadcasted_iota(jnp.int32, (MLA_TQ, MLA_TK), 0)
    col = lax.broadcasted_iota(jnp.int32, (MLA_TQ, MLA_TK), 1)

    def step(j, carry, masked):
        ks = pl.multiple_of(j * MLA_TK, MLA_TK)
        if masked:
            keep = (col - row) <= (t0 - ks)
        new = []
        for h in range(N_HEADS):
            m, l, acc = carry[h]
            qh = q_ref[0, :, h * LANE:(h + 1) * LANE]
            kt = k_ref[0, pl.ds(ks, MLA_TK), h * LANE:(h + 1) * LANE]
            vt = v_ref[0, pl.ds(ks, MLA_TK), h * HEAD_DIM:(h + 1) * HEAD_DIM]
            s = _dot_nt(qh, kt)
            if masked:
                s = jnp.where(keep, s, NEG)
            m_new = jnp.maximum(m, jnp.max(s, axis=-1, keepdims=True))
            alpha = jnp.exp(m - m_new)
            p = jnp.exp(s - m_new)
            l = alpha * l + jnp.sum(p, axis=-1, keepdims=True)
            acc = alpha * acc + _dot(p.astype(BF16), vt)
            new.append((m_new, l, acc))
        return tuple(new)

    init = tuple((jnp.full((MLA_TQ, 1), NEG, F32), jnp.zeros((MLA_TQ, 1), F32),
                  jnp.zeros((MLA_TQ, HEAD_DIM), F32)) for _ in range(N_HEADS))
    nfull = t0 // MLA_TK
    carry = lax.fori_loop(0, nfull, functools.partial(step, masked=False), init)
    carry = step(nfull, carry, True)
    o_ref[0] = jnp.concatenate([acc / l for (_, l, acc) in carry], axis=-1).astype(BF16)


def _mla_flash(q3, k3, v3):
    b, s, hq = q3.shape
    return pl.pallas_call(
        _mla_flash_kernel,
        grid=(b, s // MLA_TQ),
        in_specs=[pl.BlockSpec((1, MLA_TQ, hq), lambda bi, i: (bi, i, 0)),
                  pl.BlockSpec((1, s, hq), lambda bi, i: (bi, 0, 0)),
                  pl.BlockSpec((1, s, BRANCH_WIDTH), lambda bi, i: (bi, 0, 0))],
        out_specs=pl.BlockSpec((1, MLA_TQ, BRANCH_WIDTH), lambda bi, i: (bi, i, 0)),
        out_shape=jax.ShapeDtypeStruct((b, s, BRANCH_WIDTH), BF16),
        compiler_params=_cparams(("arbitrary", "arbitrary")),
        name="mla_flash",
    )(q3, k3, v3)


def _nsa_cmp_kernel(ch_ref, pt_ref, pb_ref, wt_ref, wb_ref, b1_ref, w2_ref, b2_ref, o_ref):
    ch = ch_ref[0]
    top = _dot(ch, wt_ref[...])
    bot = _dot(ch, wb_ref[...])
    const = (_dot(pt_ref[...], wt_ref[...]) + _dot(pb_ref[...], wb_ref[...]))[0:1, :] + b1_ref[...]
    n = ch.shape[0]
    hid = top + pltpu.roll(bot, n - 1, 0) + const
    hid = jax.nn.gelu(hid)
    o_ref[0] = (_dot(hid.astype(BF16), w2_ref[...]) + b2_ref[...]).astype(BF16)


def _nsa_compress(kvc3, pos_t, pos_b, wt, wb, b1, w2, b2):
    b, s, _ = kvc3.shape
    nch = s // NSA_CMP_STRIDE
    cw = NSA_CMP_STRIDE * 128
    ch = kvc3.reshape(b, nch, cw)
    return pl.pallas_call(
        _nsa_cmp_kernel,
        grid=(b,),
        in_specs=[pl.BlockSpec((1, nch, cw), lambda bi: (bi, 0, 0)),
                  _const_spec((8, cw)), _const_spec((8, cw)),
                  _const_spec((cw, 2 * NSA_CMP_HIDDEN)), _const_spec((cw, 2 * NSA_CMP_HIDDEN)),
                  _const_spec((1, 2 * NSA_CMP_HIDDEN)),
                  _const_spec((2 * NSA_CMP_HIDDEN, 2 * NSA_KV_DIM)), _const_spec((1, 2 * NSA_KV_DIM))],
        out_specs=pl.BlockSpec((1, nch, 2 * NSA_KV_DIM), lambda bi: (bi, 0, 0)),
        out_shape=jax.ShapeDtypeStruct((b, nch, 2 * NSA_KV_DIM), BF16),
        compiler_params=_cparams(("arbitrary",)),
        name="nsa_compress",
    )(ch, pos_t, pos_b, wt, wb, b1, w2, b2)


def _split_dot_nt(w, x, terms=3):
    out = None
    rem = x
    for _ in range(terms):
        part = rem.astype(BF16)
        rem = rem - part.astype(F32)
        y = _dot_nt(w, part)
        out = y if out is None else out + y
    return out


def _nsa_sel_table(s):
    t = np.arange(s)
    tab = np.zeros((s, 2 * LANE), np.float32)
    tab[:, HEAD_DIM] = t >> 7
    tab[:, HEAD_DIM + 1] = t & 127
    tab[t, LANE + t // NSA_SEL_BLOCK] = NEG
    return jnp.asarray(tab, dtype=BF16)


def _nsa_kernel(q_ref, g_ref, cmp_ref, sel_ref, win_ref, tab_ref, o_ref, *, ncmp):
    i = pl.program_id(1)
    t0 = i * TILE
    hr = N_HEADS * TILE
    q = q_ref[0]
    q4 = jnp.concatenate([q[:, h * HEAD_DIM:(h + 1) * HEAD_DIM] for h in range(N_HEADS)], axis=0)
    rowh = lax.broadcasted_iota(jnp.int32, (hr, 1), 0)
    hidx = rowh >> 7
    r = rowh & (TILE - 1)
    slope = jnp.where(hidx == 0, SLOPES_C[0],
                      jnp.where(hidx == 1, SLOPES_C[1], jnp.where(hidx == 2, SLOPES_C[2], SLOPES_C[3]))).astype(F32)

    def masked_softmax(s, mask):
        sb = jnp.where(mask, s, NEG)
        m = jnp.max(sb, axis=-1, keepdims=True)
        e = jnp.where(mask, jnp.exp(sb - m), 0.0)
        return e / jnp.maximum(jnp.sum(e, axis=-1, keepdims=True), 1e-30)

    kc = cmp_ref[0, :, 0:NSA_KV_DIM]
    vc = cmp_ref[0, :, NSA_KV_DIM:2 * NSA_KV_DIM]
    ncol = lax.broadcasted_iota(jnp.int32, (1, ncmp), 1)
    dist_c = (t0 + r) - (ncol * NSA_CMP_STRIDE + (NSA_CMP_BLOCK - 1))
    s = _dot_nt(q4, kc) - slope * dist_c.astype(F32)
    p_cmp = masked_softmax(s, dist_c >= 0)
    o_cmp = _dot(p_cmp.astype(BF16), vc)
    psum = p_cmp[0:TILE] + p_cmp[TILE:2 * TILE] + p_cmp[2 * TILE:3 * TILE] + p_cmp[3 * TILE:4 * TILE]
    jrow = lax.broadcasted_iota(jnp.int32, (TILE, ncmp), 0)
    ncol2 = lax.broadcasted_iota(jnp.int32, (TILE, ncmp), 1)
    delta = jrow * (NSA_SEL_BLOCK // NSA_CMP_STRIDE) - ncol2
    selmap = jnp.where((delta == 0) | (delta == 4), 1.0,
                       jnp.where((delta > 0) & (delta < 4), 2.0, 0.0)).astype(BF16)
    p_sel = _split_dot_nt(selmap, psum)

    blk = lax.broadcasted_iota(jnp.int32, (TILE, TILE), 0)
    tq = t0 + lax.broadcasted_iota(jnp.int32, (TILE, TILE), 1)
    cur = tq >> 6
    forced = (blk == 0) | (blk == cur) | (blk == cur - 1)
    cand = jnp.where((blk <= cur) & jnp.logical_not(forced), p_sel, -1.0)
    picked = jnp.where(forced, 1.0, 0.0)
    for _ in range(NSA_TOP_N - 3):
        m = jnp.max(cand, axis=0, keepdims=True)
        idx = jnp.min(jnp.where(cand == m, blk, TILE), axis=0, keepdims=True)
        hit = (blk == idx) & (m >= 0.0)
        picked = jnp.where(hit, 1.0, picked)
        cand = jnp.where(hit, -1.0, cand)
    not_sel = jnp.transpose(1.0 - picked).astype(BF16)

    lane64 = lax.broadcasted_iota(jnp.int32, (hr, HEAD_DIM), 1)
    alibi_q = jnp.where(lane64 == 0, slope * 128.0, jnp.where(lane64 == 1, slope, 0.0)).astype(BF16)
    q4aug = jnp.concatenate([q4, alibi_q, jnp.concatenate([not_sel] * N_HEADS, axis=0)], axis=1)
    keep_k = jnp.where(lax.broadcasted_iota(jnp.int32, (1, LANE), 1) < NSA_KV_DIM, 1.0, 0.0).astype(BF16)
    colk = lax.broadcasted_iota(jnp.int32, (hr, NSA_TK), 1)

    def sel_step(j, carry, diag):
        m, l, acc = carry
        ks = pl.multiple_of(j * NSA_TK, NSA_TK)
        kv = sel_ref[0, pl.ds(ks, NSA_TK), :]
        tb = tab_ref[pl.ds(ks, NSA_TK), :]
        kaug = jnp.concatenate([kv * keep_k + tb[:, 0:LANE], tb[:, LANE:2 * LANE]], axis=1)
        s = _dot_nt(q4aug, kaug)
        if diag:
            s = jnp.where((colk - r) <= (t0 - ks), s, NEG)
        m_new = jnp.maximum(m, jnp.max(s, axis=-1, keepdims=True))
        alpha = jnp.exp(m - m_new)
        p = jnp.exp(s - m_new)
        l = alpha * l + jnp.sum(p, axis=-1, keepdims=True)
        acc = alpha * acc + _dot(p.astype(BF16), kv)
        return m_new, l, acc

    init = (jnp.full((hr, 1), NEG, F32), jnp.zeros((hr, 1), F32), jnp.zeros((hr, LANE), F32))
    jd = t0 // NSA_TK
    carry = sel_step(jd, init, True)
    _, l, acc = lax.fori_loop(0, jd, functools.partial(sel_step, diag=False), carry)
    o_slc = acc[:, NSA_KV_DIM:2 * NSA_KV_DIM] / jnp.maximum(l, 1e-30)

    wlen = NSA_WINDOW + TILE
    kstart = pl.multiple_of(jnp.maximum(t0 - NSA_WINDOW, 0), TILE)
    kw = win_ref[0, pl.ds(kstart, wlen), 0:NSA_KV_DIM]
    vw = win_ref[0, pl.ds(kstart, wlen), NSA_KV_DIM:2 * NSA_KV_DIM]
    colw = lax.broadcasted_iota(jnp.int32, (hr, wlen), 1)
    dist_w = (t0 - kstart) + r - colw
    s = _dot_nt(q4, kw) - slope * dist_w.astype(F32)
    p_win = masked_softmax(s, (dist_w >= 0) & (dist_w < NSA_WINDOW))
    o_win = _dot(p_win.astype(BF16), vw)

    g = jax.nn.sigmoid(g_ref[0].astype(F32))

    def gate_col(kk):
        return jnp.concatenate([g[:, 3 * h + kk:3 * h + kk + 1] for h in range(N_HEADS)], axis=0)

    out4 = gate_col(0) * o_cmp + gate_col(1) * o_slc + gate_col(2) * o_win
    o_ref[0] = jnp.concatenate([out4[h * TILE:(h + 1) * TILE] for h in range(N_HEADS)], axis=-1).astype(BF16)


def _nsa_attention(c3, cmp3):
    b, s, _ = c3.shape
    ncmp = cmp3.shape[1]
    return pl.pallas_call(
        functools.partial(_nsa_kernel, ncmp=ncmp),
        grid=(b, s // TILE),
        in_specs=[pl.BlockSpec((1, TILE, BRANCH_WIDTH), lambda bi, i: (bi, i, C_CQ // BRANCH_WIDTH)),
                  pl.BlockSpec((1, TILE, 128), lambda bi, i: (bi, i, C_CG // 128)),
                  pl.BlockSpec((1, ncmp, 128), lambda bi, i: (bi, 0, 0)),
                  pl.BlockSpec((1, s, 128), lambda bi, i: (bi, 0, C_CSEL // 128)),
                  pl.BlockSpec((1, s, 128), lambda bi, i: (bi, 0, C_CWIN // 128)),
                  _const_spec((s, 2 * LANE))],
        out_specs=pl.BlockSpec((1, TILE, BRANCH_WIDTH), lambda bi, i: (bi, i, 0)),
        out_shape=jax.ShapeDtypeStruct((b, s, BRANCH_WIDTH), BF16),
        compiler_params=_cparams(("arbitrary", "arbitrary")),
        name="nsa_attn",
    )(c3, c3, cmp3, c3, c3, _nsa_sel_table(s))


def _sb_kernel(q_ref, k_ref, v_ref, o_ref, run_ref, acc_ref):
    i = pl.program_id(1)
    t0 = i * SB_TQ
    row = lax.broadcasted_iota(jnp.int32, (SB_TQ, TILE), 0)
    col = lax.broadcasted_iota(jnp.int32, (SB_TQ, TILE), 1)
    urow = lax.broadcasted_iota(jnp.int32, (2 * TILE, 2 * TILE), 0) & (TILE - 1)
    ucol = lax.broadcasted_iota(jnp.int32, (2 * TILE, 2 * TILE), 1)
    umat = jnp.where((ucol >= TILE) | (urow > ucol), 1.0, 0.0).astype(BF16)
    run_ref[...] = jnp.zeros_like(run_ref)
    acc_ref[...] = jnp.zeros_like(acc_ref)

    def tile(j, masked):
        ks = pl.multiple_of(j * TILE, TILE)
        if masked:
            strict = (col - row) < (t0 - ks)
        for h in range(N_HEADS):
            hs = slice(h * HEAD_DIM, (h + 1) * HEAD_DIM)
            z = _dot_nt(q_ref[0, :, hs], k_ref[0, pl.ds(ks, TILE), hs])
            lp = jnp.log1p(jnp.exp(-jnp.abs(z)))
            ls = jnp.minimum(z, 0.0) - lp
            l1m = -jnp.maximum(z, 0.0) - lp
            if masked:
                l1m = jnp.where(strict, l1m, 0.0)
            hi = l1m.astype(BF16)
            lo = (l1m - hi.astype(F32)).astype(BF16)
            cs = _dot(jnp.concatenate([hi, lo], axis=-1), umat)
            run = run_ref[h]
            a = jnp.exp(ls + run + cs[:, 0:TILE])
            if masked:
                a = jnp.where(strict, a, 0.0)
            acc_ref[h] += _dot(a.astype(BF16), v_ref[0, pl.ds(ks, TILE), hs])
            run_ref[h] = run + cs[:, TILE:2 * TILE]

    def run_max():
        return jnp.max(jnp.maximum(jnp.maximum(run_ref[0], run_ref[1]), jnp.maximum(run_ref[2], run_ref[3])))

    j_top = t0 // TILE + SB_TQ // TILE - 1
    for dj in range(SB_TQ // TILE):
        tile(j_top - dj, True)

    def cond(c):
        return (c[0] >= 0) & (c[1] > SB_UNDERFLOW)

    def body(c):
        tile(c[0], False)
        return c[0] - 1, run_max()

    lax.while_loop(cond, body, (j_top - SB_TQ // TILE, run_max()))
    o_ref[0] = jnp.concatenate([acc_ref[h] for h in range(N_HEADS)], axis=-1).astype(BF16)


def _sb_attention(c3):
    b, s, _ = c3.shape
    return pl.pallas_call(
        _sb_kernel,
        grid=(b, s // SB_TQ),
        in_specs=[pl.BlockSpec((1, SB_TQ, BRANCH_WIDTH), lambda bi, i: (bi, i, C_DQ // BRANCH_WIDTH)),
                  pl.BlockSpec((1, s, BRANCH_WIDTH), lambda bi, i: (bi, 0, C_DK // BRANCH_WIDTH)),
                  pl.BlockSpec((1, s, BRANCH_WIDTH), lambda bi, i: (bi, 0, C_DV // BRANCH_WIDTH))],
        out_specs=pl.BlockSpec((1, SB_TQ, BRANCH_WIDTH), lambda bi, i: (bi, i, 0)),
        out_shape=jax.ShapeDtypeStruct((b, s, BRANCH_WIDTH), BF16),
        scratch_shapes=[pltpu.VMEM((N_HEADS, SB_TQ, TILE), F32), pltpu.VMEM((N_HEADS, SB_TQ, HEAD_DIM), F32)],
        compiler_params=_cparams(("arbitrary", "arbitrary")),
        name="stickbreak",
    )(c3, c3, c3)


def _merge_kernel(x_ref, ng_ref, o1_ref, o2_ref, o3_ref, l1_ref, l2_ref, l3_ref, ga_ref,
                  yb_ref, gb_ref, yc_ref, gc_ref, yd_ref, gd_ref,
                  wm_ref, bm_ref, wb_ref, wo_ref, fg_ref, out_ref, *, final):
    x = x_ref[...]
    ms = jnp.mean(x * x, axis=-1, keepdims=True)
    h = (x * lax.rsqrt(ms + RMS_EPS) * ng_ref[...]).astype(BF16)
    l1, l2, l3 = l1_ref[...], l2_ref[...], l3_ref[...]
    lm = jnp.maximum(jnp.maximum(l1, l2), l3)
    e1, e2, e3 = jnp.exp(l1 - lm), jnp.exp(l2 - lm), jnp.exp(l3 - lm)
    ya = (e1 * o1_ref[...].astype(F32) + e2 * o2_ref[...].astype(F32) + e3 * o3_ref[...].astype(F32)) / (e1 + e2 + e3)
    ys = (ya, yb_ref[...].astype(F32), yc_ref[...].astype(F32), yd_ref[...].astype(F32))
    gs = (ga_ref, gb_ref, gc_ref, gd_ref)
    merged = None
    for i in range(4):
        y = (ys[i] * jax.nn.silu(gs[i][...].astype(F32))).astype(BF16)
        gate = jax.nn.sigmoid(_dot(h, wm_ref[i]) + bm_ref[i])
        term = gate * _dot(y, wb_ref[i])
        merged = term if merged is None else merged + term
    xn = x + _dot(merged.astype(BF16), wo_ref[...])
    if final:
        ms = jnp.mean(xn * xn, axis=-1, keepdims=True)
        xn = xn * lax.rsqrt(ms + RMS_EPS) * fg_ref[...]
    out_ref[...] = xn


def _merge(x2, ng, a_outs, a_lses, c2, yb, yc, yd, wm, bm, wb, wo, fg, final):
    n, d = x2.shape
    tm = 256
    bw = BRANCH_WIDTH

    def rows(width):
        return pl.BlockSpec((tm, width), lambda i: (i, 0))

    def cspec(off):
        return pl.BlockSpec((tm, bw), lambda i: (i, off // bw))

    return pl.pallas_call(
        functools.partial(_merge_kernel, final=final),
        grid=(n // tm,),
        in_specs=[rows(d), _const_spec((1, d)),
                  rows(bw), rows(bw), rows(bw), rows(bw), rows(bw), rows(bw), cspec(C_AGATE),
                  rows(bw), cspec(C_BGATE), rows(bw), cspec(C_CGATE), rows(bw), cspec(C_DGATE),
                  _const_spec((4, d, d)), _const_spec((4, 1, d)), _const_spec((4, bw, d)),
                  _const_spec((d, d)), _const_spec((1, d))],
        out_specs=rows(d),
        out_shape=jax.ShapeDtypeStruct((n, d), F32),
        compiler_params=_cparams(("arbitrary",)),
        name="merge",
    )(x2, ng, *a_outs, *a_lses, c2, yb, c2, yc, c2, yd, c2, wm, bm, wb, wo, fg)


def _split_w_in(w):
    widths = (256, 256, 256, 256, MLA_Q_RANK, MLA_KV_RANK, MLA_ROPE_DIM, 256,
              256, 64, 64, 64, 64, 64, 64, 3 * N_HEADS, 256, 256, 256, 256, 256)
    names = ("a_q", "a_k", "a_v", "a_gate", "b_cq", "b_ckv", "b_kpe", "b_gate",
             "c_q", "c_kc", "c_vc", "c_ks", "c_vs", "c_kw", "c_vw", "c_g", "c_gate",
             "d_q", "d_k", "d_v", "d_gate")
    out, off = {}, 0
    for nm, wd in zip(names, widths):
        out[nm] = w[:, off:off + wd]
        off += wd
    return out


def _rot_cols(w):
    half = w.shape[1] // 2
    return jnp.concatenate([-w[:, half:], w[:, :half]], axis=1)


def _prep_w_in(w):
    p = _split_w_in(w)
    d = w.shape[0]
    qs = HEAD_DIM ** -0.5

    def z(n):
        return jnp.zeros((d, n), w.dtype)

    kpe_blk = jnp.concatenate([z(ROPE_LANE), p["b_kpe"], z(LANE - ROPE_LANE - MLA_ROPE_DIM)], axis=1)
    kper_blk = jnp.concatenate([z(ROPE_LANE), _rot_cols(p["b_kpe"]), z(LANE - ROPE_LANE - MLA_ROPE_DIM)], axis=1)
    cols = [p["a_q"] * qs, p["a_k"], p["a_v"],
            p["a_gate"], p["b_cq"], z(256 - MLA_Q_RANK), p["b_ckv"], kpe_blk, kper_blk,
            p["c_g"], z(128 - 3 * N_HEADS), p["b_gate"],
            p["c_q"] * qs, p["c_ks"], p["c_vs"], p["c_kw"], p["c_vw"], p["c_gate"],
            p["d_q"] * qs, p["d_k"], p["d_v"], p["d_gate"],
            p["c_kc"], p["c_vc"]]
    out = jnp.concatenate(cols, axis=1)
    assert out.shape[1] == W_WIDTH
    return out.astype(BF16)


def _prep_mla(w_uq, w_ukv, gq):
    qd = MLA_NOPE_DIM + MLA_ROPE_DIM
    pad = LANE - qd
    wl, wr, wk, wv = [], [], [], []
    for h in range(N_HEADS):
        nope = w_uq[:, h * qd:h * qd + MLA_NOPE_DIM]
        rp = w_uq[:, h * qd + MLA_NOPE_DIM:(h + 1) * qd]
        zq = jnp.zeros((MLA_Q_RANK, pad), w_uq.dtype)
        wl += [nope, rp, zq]
        wr += [jnp.zeros_like(nope), _rot_cols(rp), zq]
        kn = w_ukv[:, h * 128:h * 128 + MLA_NOPE_DIM]
        wk += [kn, jnp.zeros((MLA_KV_RANK, LANE - MLA_NOPE_DIM), w_ukv.dtype)]
        wv.append(w_ukv[:, h * 128 + MLA_NOPE_DIM:(h + 1) * 128])
    rpad = ((0, 256 - MLA_Q_RANK), (0, 0))
    wl = jnp.pad(jnp.concatenate(wl, axis=1), rpad).astype(BF16)
    wr = jnp.pad(jnp.concatenate(wr, axis=1), rpad).astype(BF16)
    gq = jnp.pad(gq, (0, 256 - MLA_Q_RANK)).reshape(1, 256)
    return wl, wr, jnp.concatenate(wk, axis=1).astype(BF16), jnp.concatenate(wv, axis=1).astype(BF16), gq


def _rope_tables(s):
    half = MLA_ROPE_DIM // 2
    inv = ROPE_THETA ** (-jnp.arange(half, dtype=F32) / half)
    ang = jnp.arange(s, dtype=F32)[:, None] * inv[None, :]
    cos, sin = jnp.cos(ang), jnp.sin(ang)
    ones = jnp.ones((s, ROPE_LANE), F32)
    zl = jnp.zeros((s, ROPE_LANE), F32)
    zr = jnp.zeros((s, LANE - ROPE_LANE - MLA_ROPE_DIM), F32)
    return (jnp.concatenate([ones, cos, cos, zr], axis=1), jnp.concatenate([zl, sin, sin, zr], axis=1))


def _prep_nsa(pos, w1, b1, w2, b2):
    half = NSA_CMP_BLOCK // 2
    eye = jnp.eye(2, dtype=w1.dtype)
    w1r = w1.reshape(2, NSA_CMP_BLOCK, NSA_KV_DIM, NSA_CMP_HIDDEN)
    posr = pos

    def comb_w(part):
        return jnp.einsum("ktdh,kj->tkdjh", part, eye).reshape(half * 2 * NSA_KV_DIM, 2 * NSA_CMP_HIDDEN)

    def comb_p(part):
        flat = jnp.transpose(part, (1, 0, 2)).reshape(1, half * 2 * NSA_KV_DIM)
        return jnp.pad(flat, ((0, 7), (0, 0))).astype(BF16)

    wt, wb = comb_w(w1r[:, :half]).astype(BF16), comb_w(w1r[:, half:]).astype(BF16)
    pt, pb = comb_p(posr[:, :half]), comb_p(posr[:, half:])
    b1c = b1.reshape(1, 2 * NSA_CMP_HIDDEN)
    zero = jnp.zeros((NSA_CMP_HIDDEN, NSA_KV_DIM), w2.dtype)
    w2c = jnp.concatenate([jnp.concatenate([w2[0], zero], axis=1),
                           jnp.concatenate([zero, w2[1]], axis=1)], axis=0).astype(BF16)
    b2c = b2.reshape(1, 2 * NSA_KV_DIM)
    return pt, pb, wt, wb, b1c, w2c, b2c


def kernel(x, norm_g, w_in, mla_q_norm, mla_w_uq, mla_kv_norm, mla_w_ukv, nsa_pos, nsa_w1, nsa_b1,
           nsa_w2, nsa_b2, w_branch, w_merge, b_merge, w_out, final_norm_g):
    b, s, d = x.shape
    depth = norm_g.shape[0]
    n = b * s
    assert s % (TILE * max(DSW_DILATIONS)) == 0 and s >= NSA_WINDOW + TILE
    cos_t, sin_t = _rope_tables(s)
    x2 = x.reshape(n, d)
    for layer in range(depth):
        a2, c2, kvc = _inproj(x2, norm_g[layer].reshape(1, d), _prep_w_in(w_in[layer]))
        c3 = c2.reshape(b, s, C_WIDTH)
        a3 = a2.reshape(b, s, A_WIDTH)
        a_res = [_dsw_attention(a3, dil) for dil in DSW_DILATIONS]
        wl, wr, wk, wv, gq = _prep_mla(mla_w_uq[layer], mla_w_ukv[layer], mla_q_norm[layer])
        qb, kb, vb = _mla_prep(c2, s, cos_t, sin_t, gq, wl, wr, mla_kv_norm[layer].reshape(1, MLA_KV_RANK), wk, wv)
        hq = N_HEADS * LANE
        yb = _mla_flash(qb.reshape(b, s, hq), kb.reshape(b, s, hq), vb.reshape(b, s, BRANCH_WIDTH))
        cmp3 = _nsa_compress(kvc.reshape(b, s, 128),
                             *_prep_nsa(nsa_pos[layer], nsa_w1[layer], nsa_b1[layer], nsa_w2[layer], nsa_b2[layer]))
        yc = _nsa_attention(c3, cmp3)
        yd = _sb_attention(c3)
        x2 = _merge(x2, norm_g[layer].reshape(1, d), [r[0] for r in a_res], [r[1] for r in a_res], c2,
                    yb.reshape(n, BRANCH_WIDTH), yc.reshape(n, BRANCH_WIDTH), yd.reshape(n, BRANCH_WIDTH),
                    w_merge[layer].astype(BF16), b_merge[layer].reshape(4, 1, d),
                    w_branch[layer].astype(BF16), w_out[layer].astype(BF16),
                    final_norm_g.reshape(1, d), final=(layer == depth - 1))
    return x2.reshape(b, s, d)
```

```python
import functools

import numpy as np
import jax
import jax.numpy as jnp
from jax import lax
from jax.experimental import pallas as pl
from jax.experimental.pallas import tpu as pltpu

F32 = jnp.float32
BF16 = jnp.bfloat16

HEAD_DIM = 64
N_HEADS = 4
BRANCH_WIDTH = N_HEADS * HEAD_DIM
RMS_EPS = 1e-6
DSW_DILATIONS = (1, 4, 16)
DSW_SPAN = 128
MLA_Q_RANK = 192
MLA_KV_RANK = 128
MLA_NOPE_DIM = 64
MLA_ROPE_DIM = 32
ROPE_THETA = 10000.0
NSA_CMP_BLOCK = 32
NSA_CMP_STRIDE = 16
NSA_CMP_HIDDEN = 128
NSA_SEL_BLOCK = 64
NSA_TOP_N = 16
NSA_WINDOW = 512
NSA_KV_DIM = 64
_ALIBI = [2.0 ** (-(i + 1)) for i in range(8)]
SLOPES_A = _ALIBI[0::2]
SLOPES_C = _ALIBI[1::2]

LANE = 128
TILE = 128
VMEM_LIMIT = 56 * 1024 * 1024
NEG = -1e30
LOG2E = 1.4426950408889634
MLA_TQ, MLA_TK = 256, 512
SB_TQ = 256
SB_UNDERFLOW = -104.0

A_WIDTH = 3 * 256
C_AGATE = 0
C_BCQ, C_BCKV, C_BKPE, C_BKPEROT = 256, 512, 640, 768
C_CG, C_BGATE, C_CQ, C_CSEL, C_CWIN, C_CGATE = 896, 1024, 1280, 1536, 1664, 1792
C_DQ, C_DK, C_DV, C_DGATE = 2048, 2304, 2560, 2816
C_WIDTH = 3072
W_C = A_WIDTH
W_KVC = A_WIDTH + C_WIDTH
W_WIDTH = W_KVC + 128
NSA_TK = 512
ROPE_LANE = 64


def _dot(a, b):
    return jnp.dot(a, b, preferred_element_type=F32)


def _dot_nt(a, b):
    return lax.dot_general(a, b, (((1,), (1,)), ((), ())), preferred_element_type=F32)


def _cparams(sem):
    return pltpu.CompilerParams(dimension_semantics=sem, vmem_limit_bytes=VMEM_LIMIT)


def _const_spec(shape):
    nd = len(shape)
    return pl.BlockSpec(shape, lambda *_: (0,) * nd)


def _inproj_kernel(x_ref, g_ref, w_ref, a_ref, c_ref, kvc_ref):
    x = x_ref[...]
    ms = jnp.mean(x * x, axis=-1, keepdims=True)
    h = (x * lax.rsqrt(ms + RMS_EPS) * g_ref[...]).astype(BF16)
    cw = 256
    for j in range(A_WIDTH // cw):
        a_ref[:, j * cw:(j + 1) * cw] = _dot(h, w_ref[:, j * cw:(j + 1) * cw]).astype(BF16)
    for j in range(C_WIDTH // cw):
        c_ref[:, j * cw:(j + 1) * cw] = _dot(h, w_ref[:, W_C + j * cw:W_C + (j + 1) * cw]).astype(BF16)
    kvc_ref[...] = _dot(h, w_ref[:, W_KVC:W_KVC + 128]).astype(BF16)


def _inproj(x2, g, w):
    n, d = x2.shape
    tm = 512
    return pl.pallas_call(
        _inproj_kernel,
        grid=(n // tm,),
        in_specs=[pl.BlockSpec((tm, d), lambda i: (i, 0)),
                  _const_spec((1, d)),
                  _const_spec((d, W_WIDTH))],
        out_specs=[pl.BlockSpec((tm, A_WIDTH), lambda i: (i, 0)),
                   pl.BlockSpec((tm, C_WIDTH), lambda i: (i, 0)),
                   pl.BlockSpec((tm, 128), lambda i: (i, 0))],
        out_shape=[jax.ShapeDtypeStruct((n, A_WIDTH), BF16),
                   jax.ShapeDtypeStruct((n, C_WIDTH), BF16),
                   jax.ShapeDtypeStruct((n, 128), BF16)],
        compiler_params=_cparams(("arbitrary",)),
        name="inproj",
    )(x2, g, w)


def _dsw_kernel(q_ref, k_ref, v_ref, o_ref, lse_ref, *, dil, lq):
    blk = pl.program_id(2)
    row = lax.broadcasted_iota(jnp.int32, (TILE, 2 * TILE), 0)
    col = lax.broadcasted_iota(jnp.int32, (TILE, 2 * TILE), 1)
    for t in range(lq // TILE):
        u0 = blk * lq + t * TILE
        kstart = pl.multiple_of(jnp.maximum(u0 - TILE, 0), TILE)
        d = (u0 - kstart) + row - col
        valid = (d >= 0) & (d <= DSW_SPAN)
        df = d.astype(F32)
        q = q_ref[0, t * TILE:(t + 1) * TILE, :]
        kk = k_ref[0, pl.ds(kstart, 2 * TILE), :]
        vv = v_ref[0, pl.ds(kstart, 2 * TILE), :]
        hss = [slice(h * HEAD_DIM, (h + 1) * HEAD_DIM) for h in range(N_HEADS)]
        ss = [_dot_nt(q[:, hs], kk[:, hs]) for hs in hss]
        ps, ls, lses = [], [], []
        for h in range(N_HEADS):
            s = jnp.where(valid, ss[h] - (SLOPES_A[h] * dil) * df, NEG)
            m = jnp.max(s, axis=-1, keepdims=True)
            p = jnp.exp(s - m)
            l = jnp.sum(p, axis=-1, keepdims=True)
            ps.append(p.astype(BF16))
            ls.append(l)
            lses.append(jnp.broadcast_to(m + jnp.log(l), (TILE, HEAD_DIM)))
        outs = [_dot(ps[h], vv[:, hss[h]]) / ls[h] for h in range(N_HEADS)]
        o_ref[0, t * TILE:(t + 1) * TILE, :] = jnp.concatenate(outs, axis=-1).astype(BF16)
        lse_ref[0, t * TILE:(t + 1) * TILE, :] = jnp.concatenate(lses, axis=-1)


def _dsw_attention(a3, dil):
    b, s, _ = a3.shape
    l = s // dil
    lq = min(512, l)
    ncb = A_WIDTH // BRANCH_WIDTH
    cv = a3.reshape(b, l, dil * A_WIDTH)
    qspec = pl.BlockSpec((1, lq, BRANCH_WIDTH), lambda bi, r, i: (bi, i, r * ncb))
    kspec = pl.BlockSpec((1, l, BRANCH_WIDTH), lambda bi, r, i: (bi, 0, r * ncb + 1))
    vspec = pl.BlockSpec((1, l, BRANCH_WIDTH), lambda bi, r, i: (bi, 0, r * ncb + 2))
    ospec = pl.BlockSpec((1, lq, BRANCH_WIDTH), lambda bi, r, i: (bi, i, r))
    o, lse = pl.pallas_call(
        functools.partial(_dsw_kernel, dil=dil, lq=lq),
        grid=(b, dil, l // lq),
        in_specs=[qspec, kspec, vspec],
        out_specs=[ospec, ospec],
        out_shape=[jax.ShapeDtypeStruct((b, l, dil * BRANCH_WIDTH), BF16),
                   jax.ShapeDtypeStruct((b, l, dil * BRANCH_WIDTH), F32)],
        compiler_params=_cparams(("arbitrary", "arbitrary", "arbitrary")),
        name=f"dsw{dil}",
    )(cv, cv, cv)
    return o.reshape(b * s, BRANCH_WIDTH), lse.reshape(b * s, BRANCH_WIDTH)


def _mla_prep_kernel(cq_ref, ckv_ref, kpe_ref, kper_ref, cos_ref, sin_ref, gq_ref, wl_ref, wr_ref,
                     gkv_ref, wk_ref, wv_ref, q_ref, k_ref, v_ref):
    cos = cos_ref[...]
    sin = sin_ref[...]
    cos4 = jnp.concatenate([cos] * N_HEADS, axis=-1)
    sin4 = jnp.concatenate([sin] * N_HEADS, axis=-1)
    cq = cq_ref[...].astype(F32)
    ms = jnp.sum(cq * cq, axis=-1, keepdims=True) * (1.0 / MLA_Q_RANK)
    qn = (cq * lax.rsqrt(ms + RMS_EPS) * gq_ref[...]).astype(BF16)
    scale = (MLA_NOPE_DIM + MLA_ROPE_DIM) ** -0.5 * LOG2E
    q = (_dot(qn, wl_ref[...]) * cos4 + _dot(qn, wr_ref[...]) * sin4) * scale
    q_ref[...] = q.astype(BF16)
    ckv = ckv_ref[...].astype(F32)
    ms = jnp.mean(ckv * ckv, axis=-1, keepdims=True)
    kn = (ckv * lax.rsqrt(ms + RMS_EPS) * gkv_ref[...]).astype(BF16)
    kpe = kpe_ref[...].astype(F32) * cos + kper_ref[...].astype(F32) * sin
    k = _dot(kn, wk_ref[...]) + jnp.concatenate([kpe] * N_HEADS, axis=-1)
    k_ref[...] = k.astype(BF16)
    lane = lax.broadcasted_iota(jnp.int32, (1, N_HEADS * LANE), 1) & (LANE - 1)
    v_ref[...] = (_dot(kn, wv_ref[...]) + jnp.where(lane == HEAD_DIM, 1.0, 0.0)).astype(BF16)


def _mla_prep(c2, s, cos_t, sin_t, gq, wl, wr, gkv, wk, wv):
    n = c2.shape[0]
    tm = 512
    spb = s // tm
    hq = N_HEADS * LANE

    def cspec(off, width):
        return pl.BlockSpec((tm, width), lambda i: (i, off // width))

    tspec = pl.BlockSpec((tm, LANE), lambda i: (i % spb, 0))
    return pl.pallas_call(
        _mla_prep_kernel,
        grid=(n // tm,),
        in_specs=[cspec(C_BCQ, 256), cspec(C_BCKV, 128), cspec(C_BKPE, 128), cspec(C_BKPEROT, 128),
                  tspec, tspec,
                  _const_spec((1, 256)), _const_spec((256, hq)), _const_spec((256, hq)),
                  _const_spec((1, 128)), _const_spec((128, hq)), _const_spec((128, hq))],
        out_specs=[pl.BlockSpec((tm, hq), lambda i: (i, 0)),
                   pl.BlockSpec((tm, hq), lambda i: (i, 0)),
                   pl.BlockSpec((tm, hq), lambda i: (i, 0))],
        out_shape=[jax.ShapeDtypeStruct((n, hq), BF16),
                   jax.ShapeDtypeStruct((n, hq), BF16),
                   jax.ShapeDtypeStruct((n, hq), BF16)],
        compiler_params=_cparams(("arbitrary",)),
        name="mla_prep",
    )(c2, c2, c2, c2, cos_t, sin_t, gq, wl, wr, gkv, wk, wv)


def _mla_flash_kernel(q_ref, k_ref, v_ref, o_ref):
    i = pl.program_id(1)
    t0 = i * MLA_TQ
    row = lax.broadcasted_iota(jnp.int32, (MLA_TQ, MLA_TK), 0)
    col = lax.broadcasted_iota(jnp.int32, (MLA_TQ, MLA_TK), 1)

    def step(j, carry, masked):
        ks = pl.multiple_of(j * MLA_TK, MLA_TK)
        if masked:
            keep = (col - row) <= (t0 - ks)
        hls = [slice(h * LANE, (h + 1) * LANE) for h in range(N_HEADS)]
        ss = [_dot_nt(q_ref[0, :, hl], k_ref[0, pl.ds(ks, MLA_TK), hl]) for hl in hls]
        ps, ms = [], []
        for h in range(N_HEADS):
            s = jnp.where(keep, ss[h], NEG) if masked else ss[h]
            m_new = jnp.maximum(carry[h][0], jnp.max(s, axis=-1, keepdims=True))
            ps.append(jnp.exp2(s - m_new).astype(BF16))
            ms.append(m_new)
        new = []
        for h in range(N_HEADS):
            m, acc = carry[h]
            acc = jnp.exp2(m - ms[h]) * acc + _dot(ps[h], v_ref[0, pl.ds(ks, MLA_TK), hls[h]])
            new.append((ms[h], acc))
        return tuple(new)

    init = tuple((jnp.full((MLA_TQ, 1), NEG, F32), jnp.zeros((MLA_TQ, LANE), F32)) for _ in range(N_HEADS))
    nfull = t0 // MLA_TK
    carry = lax.fori_loop(0, nfull, functools.partial(step, masked=False), init)
    carry = step(nfull, carry, True)
    o_ref[0] = jnp.concatenate([acc[:, 0:HEAD_DIM] / acc[:, HEAD_DIM:HEAD_DIM + 1] for (_, acc) in carry],
                               axis=-1).astype(BF16)


def _mla_flash(q3, k3, v3):
    b, s, hq = q3.shape
    return pl.pallas_call(
        _mla_flash_kernel,
        grid=(b, s // MLA_TQ),
        in_specs=[pl.BlockSpec((1, MLA_TQ, hq), lambda bi, i: (bi, i, 0)),
                  pl.BlockSpec((1, s, hq), lambda bi, i: (bi, 0, 0)),
                  pl.BlockSpec((1, s, hq), lambda bi, i: (bi, 0, 0))],
        out_specs=pl.BlockSpec((1, MLA_TQ, BRANCH_WIDTH), lambda bi, i: (bi, i, 0)),
        out_shape=jax.ShapeDtypeStruct((b, s, BRANCH_WIDTH), BF16),
        compiler_params=_cparams(("arbitrary", "arbitrary")),
        name="mla_flash",
    )(q3, k3, v3)


def _nsa_cmp_kernel(ch_ref, pt_ref, pb_ref, wt_ref, wb_ref, b1_ref, w2_ref, b2_ref, o_ref):
    ch = ch_ref[0]
    top = _dot(ch, wt_ref[...])
    bot = _dot(ch, wb_ref[...])
    const = (_dot(pt_ref[...], wt_ref[...]) + _dot(pb_ref[...], wb_ref[...]))[0:1, :] + b1_ref[...]
    n = ch.shape[0]
    hid = top + pltpu.roll(bot, n - 1, 0) + const
    hid = jax.nn.gelu(hid)
    o_ref[0] = (_dot(hid.astype(BF16), w2_ref[...]) + b2_ref[...]).astype(BF16)


def _nsa_compress(kvc3, pos_t, pos_b, wt, wb, b1, w2, b2):
    b, s, _ = kvc3.shape
    nch = s // NSA_CMP_STRIDE
    cw = NSA_CMP_STRIDE * 128
    ch = kvc3.reshape(b, nch, cw)
    return pl.pallas_call(
        _nsa_cmp_kernel,
        grid=(b,),
        in_specs=[pl.BlockSpec((1, nch, cw), lambda bi: (bi, 0, 0)),
                  _const_spec((8, cw)), _const_spec((8, cw)),
                  _const_spec((cw, 2 * NSA_CMP_HIDDEN)), _const_spec((cw, 2 * NSA_CMP_HIDDEN)),
                  _const_spec((1, 2 * NSA_CMP_HIDDEN)),
                  _const_spec((2 * NSA_CMP_HIDDEN, 2 * NSA_KV_DIM)), _const_spec((1, 2 * NSA_KV_DIM))],
        out_specs=pl.BlockSpec((1, nch, 2 * NSA_KV_DIM), lambda bi: (bi, 0, 0)),
        out_shape=jax.ShapeDtypeStruct((b, nch, 2 * NSA_KV_DIM), BF16),
        compiler_params=_cparams(("arbitrary",)),
        name="nsa_compress",
    )(ch, pos_t, pos_b, wt, wb, b1, w2, b2)


def _split_dot_nt(w, x, terms=3):
    out = None
    rem = x
    for _ in range(terms):
        part = rem.astype(BF16)
        rem = rem - part.astype(F32)
        y = _dot_nt(w, part)
        out = y if out is None else out + y
    return out


def _nsa_sel_table(s):
    t = np.arange(s)
    tab = np.zeros((s, 2 * LANE), np.float32)
    tab[:, HEAD_DIM] = t >> 7
    tab[:, HEAD_DIM + 1] = t & 127
    tab[t, LANE + t // NSA_SEL_BLOCK] = NEG
    return jnp.asarray(tab, dtype=BF16)


def _nsa_cmp_table(ncmp):
    end = np.arange(ncmp) * NSA_CMP_STRIDE + NSA_CMP_BLOCK - 1
    tab = np.zeros((ncmp, LANE), np.float32)
    tab[:, HEAD_DIM] = end >> 7
    tab[:, HEAD_DIM + 1] = end & 127
    return jnp.asarray(tab, dtype=BF16)


def _nsa_kernel(q_ref, g_ref, cmp_ref, sel_ref, win_ref, tab_ref, ctab_ref, o_ref, *, ncmp):
    i = pl.program_id(1)
    t0 = i * TILE
    hr = N_HEADS * TILE
    q = q_ref[0]
    q4 = jnp.concatenate([q[:, h * HEAD_DIM:(h + 1) * HEAD_DIM] for h in range(N_HEADS)], axis=0)
    rowh = lax.broadcasted_iota(jnp.int32, (hr, 1), 0)
    hidx = rowh >> 7
    r = rowh & (TILE - 1)
    slope = jnp.where(hidx == 0, SLOPES_C[0],
                      jnp.where(hidx == 1, SLOPES_C[1], jnp.where(hidx == 2, SLOPES_C[2], SLOPES_C[3]))).astype(F32)

    lane64 = lax.broadcasted_iota(jnp.int32, (hr, HEAD_DIM), 1)
    alibi_q = jnp.where(lane64 == 0, slope * 128.0, jnp.where(lane64 == 1, slope, 0.0)).astype(BF16)
    q4a = jnp.concatenate([q4, alibi_q], axis=1)
    lane = lax.broadcasted_iota(jnp.int32, (1, LANE), 1)
    keep_k = jnp.where(lane < NSA_KV_DIM, 1.0, 0.0).astype(BF16)
    keep_v = jnp.where(lane >= NSA_KV_DIM, 1.0, 0.0).astype(BF16)
    one0 = jnp.where(lane == 0, 1.0, 0.0).astype(BF16)

    def value_operand(kv):
        return kv * keep_v + one0

    def masked_softmax(s, mask):
        sb = jnp.where(mask, s, NEG)
        m = jnp.maximum(jnp.max(sb, axis=-1, keepdims=True), 0.1 * NEG)
        return jnp.exp(sb - m)

    def normalise(pv):
        return pv[:, NSA_KV_DIM:2 * NSA_KV_DIM], jnp.maximum(pv[:, 0:1], 1e-30)

    ckv = cmp_ref[0]
    ncol = lax.broadcasted_iota(jnp.int32, (1, ncmp), 1)
    last_ok = (t0 + r - (NSA_CMP_BLOCK - 1)) >> 4
    s_cmp = _dot_nt(q4a, ckv * keep_k + ctab_ref[...])
    wlen = NSA_WINDOW + TILE
    kstart = pl.multiple_of(jnp.maximum(t0 - NSA_WINDOW, 0), TILE)
    wkv = win_ref[0, pl.ds(kstart, wlen), :]
    s_win = _dot_nt(q4a, wkv * keep_k + tab_ref[pl.ds(kstart, wlen), 0:LANE])
    e_cmp = masked_softmax(s_cmp, ncol <= last_ok)
    den_c = jnp.maximum(jnp.sum(e_cmp, axis=-1, keepdims=True), 1e-30)
    p_cmp = e_cmp / den_c
    o_cmp = _dot(p_cmp.astype(BF16), ckv)[:, NSA_KV_DIM:2 * NSA_KV_DIM]
    psum =p_cmp[0:TILE] + p_cmp[TILE:2 * TILE] + p_cmp[2 * TILE:3 * TILE] + p_cmp[3 * TILE:4 * TILE]
    jrow = lax.broadcasted_iota(jnp.int32, (TILE, ncmp), 0)
    ncol2 = lax.broadcasted_iota(jnp.int32, (TILE, ncmp), 1)
    delta = jrow * (NSA_SEL_BLOCK // NSA_CMP_STRIDE) - ncol2
    selmap = jnp.where((delta == 0) | (delta == 4), 1.0,
                       jnp.where((delta > 0) & (delta < 4), 2.0, 0.0)).astype(BF16)
    p_sel = _split_dot_nt(selmap, psum)

    blk = lax.broadcasted_iota(jnp.int32, (TILE, TILE), 0)
    tq = t0 + lax.broadcasted_iota(jnp.int32, (TILE, TILE), 1)
    cur = tq >> 6
    forced = (blk == 0) | (blk == cur) | (blk == cur - 1)
    cand = jnp.where((blk <= cur) & jnp.logical_not(forced), p_sel, -1.0)
    picked = jnp.where(forced, 1.0, 0.0)
    for _ in range(NSA_TOP_N - 3):
        m = jnp.max(cand, axis=0, keepdims=True)
        idx = jnp.min(jnp.where(cand == m, blk, TILE), axis=0, keepdims=True)
        hit = (blk == idx) & (m >= 0.0)
        picked = jnp.where(hit, 1.0, picked)
        cand = jnp.where(hit, -1.0, cand)
    not_sel = jnp.transpose(1.0 - picked).astype(BF16)

    q4aug = jnp.concatenate([q4a, jnp.concatenate([not_sel] * N_HEADS, axis=0)], axis=1)
    colk = lax.broadcasted_iota(jnp.int32, (hr, NSA_TK), 1)

    def sel_step(j, carry, diag):
        m, acc = carry
        ks = pl.multiple_of(j * NSA_TK, NSA_TK)
        kv = sel_ref[0, pl.ds(ks, NSA_TK), :]
        tb = tab_ref[pl.ds(ks, NSA_TK), :]
        kaug = jnp.concatenate([kv * keep_k + tb[:, 0:LANE], tb[:, LANE:2 * LANE]], axis=1)
        s = _dot_nt(q4aug, kaug)
        if diag:
            s = jnp.where((colk - r) <= (t0 - ks), s, NEG)
        m_new = jnp.maximum(m, jnp.max(s, axis=-1, keepdims=True))
        p = jnp.exp(s - m_new)
        acc = jnp.exp(m - m_new) * acc + _dot(p.astype(BF16), value_operand(kv))
        return m_new, acc

    init = (jnp.full((hr, 1), NEG, F32), jnp.zeros((hr, LANE), F32))
    jd = t0 // NSA_TK
    carry = sel_step(jd, init, True)
    _, acc = lax.fori_loop(0, jd, functools.partial(sel_step, diag=False), carry)
    o_slc, den_s = normalise(acc)
    o_slc = o_slc / den_s

    colw = lax.broadcasted_iota(jnp.int32, (hr, wlen), 1)
    dist_w = (t0 - kstart) + r - colw
    e_win = masked_softmax(s_win, (dist_w >= 0) & (dist_w < NSA_WINDOW))
    o_win, den_w = normalise(_dot(e_win.astype(BF16), value_operand(wkv)))
    o_win = o_win / den_w

    g = jax.nn.sigmoid(g_ref[0].astype(F32))

    def gate_col(kk):
        return jnp.concatenate([g[:, 3 * h + kk:3 * h + kk + 1] for h in range(N_HEADS)], axis=0)

    out4 = gate_col(0) * o_cmp + gate_col(1) * o_slc + gate_col(2) * o_win
    o_ref[0] = jnp.concatenate([out4[h * TILE:(h + 1) * TILE] for h in range(N_HEADS)], axis=-1).astype(BF16)


def _nsa_attention(c3, cmp3):
    b, s, _ = c3.shape
    ncmp = cmp3.shape[1]
    return pl.pallas_call(
        functools.partial(_nsa_kernel, ncmp=ncmp),
        grid=(b, s // TILE),
        in_specs=[pl.BlockSpec((1, TILE, BRANCH_WIDTH), lambda bi, i: (bi, i, C_CQ // BRANCH_WIDTH)),
                  pl.BlockSpec((1, TILE, 128), lambda bi, i: (bi, i, C_CG // 128)),
                  pl.BlockSpec((1, ncmp, 128), lambda bi, i: (bi, 0, 0)),
                  pl.BlockSpec((1, s, 128), lambda bi, i: (bi, 0, C_CSEL // 128)),
                  pl.BlockSpec((1, s, 128), lambda bi, i: (bi, 0, C_CWIN // 128)),
                  _const_spec((s, 2 * LANE)), _const_spec((ncmp, LANE))],
        out_specs=pl.BlockSpec((1, TILE, BRANCH_WIDTH), lambda bi, i: (bi, i, 0)),
        out_shape=jax.ShapeDtypeStruct((b, s, BRANCH_WIDTH), BF16),
        compiler_params=_cparams(("arbitrary", "arbitrary")),
        name="nsa_attn",
    )(c3, c3, cmp3, c3, c3, _nsa_sel_table(s), _nsa_cmp_table(ncmp))


def _sb_kernel(q_ref, k_ref, v_ref, o_ref, run_ref, acc_ref):
    i = pl.program_id(1)
    t0 = i * SB_TQ
    row = lax.broadcasted_iota(jnp.int32, (SB_TQ, TILE), 0)
    col = lax.broadcasted_iota(jnp.int32, (SB_TQ, TILE), 1)
    urow = lax.broadcasted_iota(jnp.int32, (2 * TILE, 2 * TILE), 0) & (TILE - 1)
    ucol = lax.broadcasted_iota(jnp.int32, (2 * TILE, 2 * TILE), 1)
    umat = jnp.where((ucol >= TILE) | (urow > ucol), 1.0, 0.0).astype(BF16)
    run_ref[...] = jnp.zeros_like(run_ref)
    acc_ref[...] = jnp.zeros_like(acc_ref)

    def tile(j, masked):
        ks = pl.multiple_of(j * TILE, TILE)
        if masked:
            strict = (col - row) < (t0 - ks)
        kt = k_ref[0, pl.ds(ks, TILE), :]
        vt = v_ref[0, pl.ds(ks, TILE), :]
        hss = [slice(h * HEAD_DIM, (h + 1) * HEAD_DIM) for h in range(N_HEADS)]
        zs = [_dot_nt(q_ref[0, :, hs], kt[:, hs]) for hs in hss]
        lss, css = [], []
        for z in zs:
            ls = jnp.minimum(z, 0.0) - jnp.log(1.0 + jnp.exp(-jnp.abs(z)))
            l1m = ls - z
            if masked:
                l1m = jnp.where(strict, l1m, 0.0)
            hi = l1m.astype(BF16)
            lo = (l1m - hi.astype(F32)).astype(BF16)
            lss.append(ls)
            css.append(_dot(jnp.concatenate([hi, lo], axis=-1), umat))
        for h in range(N_HEADS):
            run = run_ref[h]
            a = jnp.exp(lss[h] + run + css[h][:, 0:TILE])
            if masked:
                a = jnp.where(strict, a, 0.0)
            acc_ref[h] += _dot(a.astype(BF16), vt[:, hss[h]])
            run_ref[h] = run + css[h][:, TILE:2 * TILE]

    def run_max():
        return jnp.max(jnp.maximum(jnp.maximum(run_ref[0], run_ref[1]), jnp.maximum(run_ref[2], run_ref[3])))

    j_top = t0 // TILE + SB_TQ // TILE - 1
    for dj in range(SB_TQ // TILE):
        tile(j_top - dj, True)

    def cond(c):
        return (c[0] >= 0) & (c[1] > SB_UNDERFLOW)

    def body(c):
        tile(c[0], False)
        return c[0] - 1, run_max()

    lax.while_loop(cond, body, (j_top - SB_TQ // TILE, run_max()))
    o_ref[0] = jnp.concatenate([acc_ref[h] for h in range(N_HEADS)], axis=-1).astype(BF16)


def _sb_attention(c3):
    b, s, _ = c3.shape
    return pl.pallas_call(
        _sb_kernel,
        grid=(b, s // SB_TQ),
        in_specs=[pl.BlockSpec((1, SB_TQ, BRANCH_WIDTH), lambda bi, i: (bi, i, C_DQ // BRANCH_WIDTH)),
                  pl.BlockSpec((1, s, BRANCH_WIDTH), lambda bi, i: (bi, 0, C_DK // BRANCH_WIDTH)),
                  pl.BlockSpec((1, s, BRANCH_WIDTH), lambda bi, i: (bi, 0, C_DV // BRANCH_WIDTH))],
        out_specs=pl.BlockSpec((1, SB_TQ, BRANCH_WIDTH), lambda bi, i: (bi, i, 0)),
        out_shape=jax.ShapeDtypeStruct((b, s, BRANCH_WIDTH), BF16),
        scratch_shapes=[pltpu.VMEM((N_HEADS, SB_TQ, TILE), F32), pltpu.VMEM((N_HEADS, SB_TQ, HEAD_DIM), F32)],
        compiler_params=_cparams(("arbitrary", "arbitrary")),
        name="stickbreak",
    )(c3, c3, c3)


def _merge_kernel(x_ref, ng_ref, o1_ref, o2_ref, o3_ref, l1_ref, l2_ref, l3_ref, ga_ref,
                  yb_ref, gb_ref, yc_ref, gc_ref, yd_ref, gd_ref,
                  wm_ref, bm_ref, wb_ref, wo_ref, fg_ref, out_ref, *, final):
    x = x_ref[...]
    ms = jnp.mean(x * x, axis=-1, keepdims=True)
    h = (x * lax.rsqrt(ms + RMS_EPS) * ng_ref[...]).astype(BF16)
    l1, l2, l3 = l1_ref[...], l2_ref[...], l3_ref[...]
    lm = jnp.maximum(jnp.maximum(l1, l2), l3)
    e1, e2, e3 = jnp.exp(l1 - lm), jnp.exp(l2 - lm), jnp.exp(l3 - lm)
    ya = (e1 * o1_ref[...].astype(F32) + e2 * o2_ref[...].astype(F32) + e3 * o3_ref[...].astype(F32)) / (e1 + e2 + e3)
    ys = (ya, yb_ref[...].astype(F32), yc_ref[...].astype(F32), yd_ref[...].astype(F32))
    gs = (ga_ref, gb_ref, gc_ref, gd_ref)
    merged = None
    for i in range(4):
        y = (ys[i] * jax.nn.silu(gs[i][...].astype(F32))).astype(BF16)
        gate = jax.nn.sigmoid(_dot(h, wm_ref[i]) + bm_ref[i])
        term = gate * _dot(y, wb_ref[i])
        merged = term if merged is None else merged + term
    xn = x + _dot(merged.astype(BF16), wo_ref[...])
    if final:
        ms = jnp.mean(xn * xn, axis=-1, keepdims=True)
        xn = xn * lax.rsqrt(ms + RMS_EPS) * fg_ref[...]
    out_ref[...] = xn


def _merge(x2, ng, a_outs, a_lses, c2, yb, yc, yd, wm, bm, wb, wo, fg, final):
    n, d = x2.shape
    tm = 256
    bw = BRANCH_WIDTH

    def rows(width):
        return pl.BlockSpec((tm, width), lambda i: (i, 0))

    def cspec(off):
        return pl.BlockSpec((tm, bw), lambda i: (i, off // bw))

    return pl.pallas_call(
        functools.partial(_merge_kernel, final=final),
        grid=(n // tm,),
        in_specs=[rows(d), _const_spec((1, d)),
                  rows(bw), rows(bw), rows(bw), rows(bw), rows(bw), rows(bw), cspec(C_AGATE),
                  rows(bw), cspec(C_BGATE), rows(bw), cspec(C_CGATE), rows(bw), cspec(C_DGATE),
                  _const_spec((4, d, d)), _const_spec((4, 1, d)), _const_spec((4, bw, d)),
                  _const_spec((d, d)), _const_spec((1, d))],
        out_specs=rows(d),
        out_shape=jax.ShapeDtypeStruct((n, d), F32),
        compiler_params=_cparams(("arbitrary",)),
        name="merge",
    )(x2, ng, *a_outs, *a_lses, c2, yb, c2, yc, c2, yd, c2, wm, bm, wb, wo, fg)


def _split_w_in(w):
    widths = (256, 256, 256, 256, MLA_Q_RANK, MLA_KV_RANK, MLA_ROPE_DIM, 256,
              256, 64, 64, 64, 64, 64, 64, 3 * N_HEADS, 256, 256, 256, 256, 256)
    names = ("a_q", "a_k", "a_v", "a_gate", "b_cq", "b_ckv", "b_kpe", "b_gate",
             "c_q", "c_kc", "c_vc", "c_ks", "c_vs", "c_kw", "c_vw", "c_g", "c_gate",
             "d_q", "d_k", "d_v", "d_gate")
    out, off = {}, 0
    for nm, wd in zip(names, widths):
        out[nm] = w[:, off:off + wd]
        off += wd
    return out


def _rot_cols(w):
    half = w.shape[1] // 2
    return jnp.concatenate([-w[:, half:], w[:, :half]], axis=1)


def _prep_w_in(w):
    p = _split_w_in(w)
    d = w.shape[0]
    qs = HEAD_DIM ** -0.5

    def z(n):
        return jnp.zeros((d, n), w.dtype)

    kpe_blk = jnp.concatenate([z(ROPE_LANE), p["b_kpe"], z(LANE - ROPE_LANE - MLA_ROPE_DIM)], axis=1)
    kper_blk = jnp.concatenate([z(ROPE_LANE), _rot_cols(p["b_kpe"]), z(LANE - ROPE_LANE - MLA_ROPE_DIM)], axis=1)
    cols = [p["a_q"] * qs, p["a_k"], p["a_v"],
            p["a_gate"], p["b_cq"], z(256 - MLA_Q_RANK), p["b_ckv"], kpe_blk, kper_blk,
            p["c_g"], z(128 - 3 * N_HEADS), p["b_gate"],
            p["c_q"] * qs, p["c_ks"], p["c_vs"], p["c_kw"], p["c_vw"], p["c_gate"],
            p["d_q"] * qs, p["d_k"], p["d_v"], p["d_gate"],
            p["c_kc"], p["c_vc"]]
    out = jnp.concatenate(cols, axis=1)
    assert out.shape[1] == W_WIDTH
    return out.astype(BF16)


def _prep_mla(w_uq, w_ukv, gq):
    qd = MLA_NOPE_DIM + MLA_ROPE_DIM
    pad = LANE - qd
    wl, wr, wk, wv = [], [], [], []
    for h in range(N_HEADS):
        nope = w_uq[:, h * qd:h * qd + MLA_NOPE_DIM]
        rp = w_uq[:, h * qd + MLA_NOPE_DIM:(h + 1) * qd]
        zq = jnp.zeros((MLA_Q_RANK, pad), w_uq.dtype)
        wl += [nope, rp, zq]
        wr += [jnp.zeros_like(nope), _rot_cols(rp), zq]
        kn = w_ukv[:, h * 128:h * 128 + MLA_NOPE_DIM]
        wk += [kn, jnp.zeros((MLA_KV_RANK, LANE - MLA_NOPE_DIM), w_ukv.dtype)]
        wv += [w_ukv[:, h * 128 + MLA_NOPE_DIM:(h + 1) * 128],
               jnp.zeros((MLA_KV_RANK, LANE - HEAD_DIM), w_ukv.dtype)]
    rpad = ((0, 256 - MLA_Q_RANK), (0, 0))
    wl = jnp.pad(jnp.concatenate(wl, axis=1), rpad).astype(BF16)
    wr = jnp.pad(jnp.concatenate(wr, axis=1), rpad).astype(BF16)
    gq = jnp.pad(gq, (0, 256 - MLA_Q_RANK)).reshape(1, 256)
    return wl, wr, jnp.concatenate(wk, axis=1).astype(BF16), jnp.concatenate(wv, axis=1).astype(BF16), gq


def _rope_tables(s):
    half = MLA_ROPE_DIM // 2
    inv = ROPE_THETA ** (-jnp.arange(half, dtype=F32) / half)
    ang = jnp.arange(s, dtype=F32)[:, None] * inv[None, :]
    cos, sin = jnp.cos(ang), jnp.sin(ang)
    ones = jnp.ones((s, ROPE_LANE), F32)
    zl = jnp.zeros((s, ROPE_LANE), F32)
    zr = jnp.zeros((s, LANE - ROPE_LANE - MLA_ROPE_DIM), F32)
    return (jnp.concatenate([ones, cos, cos, zr], axis=1), jnp.concatenate([zl, sin, sin, zr], axis=1))


def _prep_nsa(pos, w1, b1, w2, b2):
    half = NSA_CMP_BLOCK // 2
    eye = jnp.eye(2, dtype=w1.dtype)
    w1r = w1.reshape(2, NSA_CMP_BLOCK, NSA_KV_DIM, NSA_CMP_HIDDEN)
    posr = pos

    def comb_w(part):
        return jnp.einsum("ktdh,kj->tkdjh", part, eye).reshape(half * 2 * NSA_KV_DIM, 2 * NSA_CMP_HIDDEN)

    def comb_p(part):
        flat = jnp.transpose(part, (1, 0, 2)).reshape(1, half * 2 * NSA_KV_DIM)
        return jnp.pad(flat, ((0, 7), (0, 0))).astype(BF16)

    wt, wb = comb_w(w1r[:, :half]).astype(BF16), comb_w(w1r[:, half:]).astype(BF16)
    pt, pb = comb_p(posr[:, :half]), comb_p(posr[:, half:])
    b1c = b1.reshape(1, 2 * NSA_CMP_HIDDEN)
    zero = jnp.zeros((NSA_CMP_HIDDEN, NSA_KV_DIM), w2.dtype)
    w2c = jnp.concatenate([jnp.concatenate([w2[0], zero], axis=1),
                           jnp.concatenate([zero, w2[1]], axis=1)], axis=0).astype(BF16)
    b2c = b2.reshape(1, 2 * NSA_KV_DIM)
    return pt, pb, wt, wb, b1c, w2c, b2c


def kernel(x, norm_g, w_in, mla_q_norm, mla_w_uq, mla_kv_norm, mla_w_ukv, nsa_pos, nsa_w1, nsa_b1,
           nsa_w2, nsa_b2, w_branch, w_merge, b_merge, w_out, final_norm_g):
    b, s, d = x.shape
    depth = norm_g.shape[0]
    n = b * s
    assert s % (TILE * max(DSW_DILATIONS)) == 0 and s >= NSA_WINDOW + TILE
    cos_t, sin_t = _rope_tables(s)
    x2 = x.reshape(n, d)
    for layer in range(depth):
        a2, c2, kvc = _inproj(x2, norm_g[layer].reshape(1, d), _prep_w_in(w_in[layer]))
        c3 = c2.reshape(b, s, C_WIDTH)
        a3 = a2.reshape(b, s, A_WIDTH)
        a_res = [_dsw_attention(a3, dil) for dil in DSW_DILATIONS]
        wl, wr, wk, wv, gq = _prep_mla(mla_w_uq[layer], mla_w_ukv[layer], mla_q_norm[layer])
        qb, kb, vb = _mla_prep(c2, s, cos_t, sin_t, gq, wl, wr, mla_kv_norm[layer].reshape(1, MLA_KV_RANK), wk, wv)
        hq = N_HEADS * LANE
        yb = _mla_flash(qb.reshape(b, s, hq), kb.reshape(b, s, hq), vb.reshape(b, s, hq))
        cmp3 = _nsa_compress(kvc.reshape(b, s, 128),
                             *_prep_nsa(nsa_pos[layer], nsa_w1[layer], nsa_b1[layer], nsa_w2[layer], nsa_b2[layer]))
        yc = _nsa_attention(c3, cmp3)
        yd = _sb_attention(c3)
        x2 = _merge(x2, norm_g[layer].reshape(1, d), [r[0] for r in a_res], [r[1] for r in a_res], c2,
                    yb.reshape(n, BRANCH_WIDTH), yc.reshape(n, BRANCH_WIDTH), yd.reshape(n, BRANCH_WIDTH),
                    w_merge[layer].astype(BF16), b_merge[layer].reshape(4, 1, d),
                    w_branch[layer].astype(BF16), w_out[layer].astype(BF16),
                    final_norm_g.reshape(1, d), final=(layer == depth - 1))
    return x2.reshape(b, s, d)
```

```python
import functools

import numpy as np
import jax
import jax.numpy as jnp
from jax import lax
from jax.experimental import pallas as pl
from jax.experimental.pallas import tpu as pltpu

F32 = jnp.float32
BF16 = jnp.bfloat16

HEAD_DIM = 64
N_HEADS = 4
BRANCH_WIDTH = N_HEADS * HEAD_DIM
RMS_EPS = 1e-6
DSW_DILATIONS = (1, 4, 16)
DSW_SPAN = 128
MLA_Q_RANK = 192
MLA_KV_RANK = 128
MLA_NOPE_DIM = 64
MLA_ROPE_DIM = 32
ROPE_THETA = 10000.0
NSA_CMP_BLOCK = 32
NSA_CMP_STRIDE = 16
NSA_CMP_HIDDEN = 128
NSA_SEL_BLOCK = 64
NSA_TOP_N = 16
NSA_WINDOW = 512
NSA_KV_DIM = 64
_ALIBI = [2.0 ** (-(i + 1)) for i in range(8)]
SLOPES_A = _ALIBI[0::2]
SLOPES_C = _ALIBI[1::2]

LANE = 128
TILE = 128
VMEM_LIMIT = 56 * 1024 * 1024
NEG = -1e30
LOG2E = 1.4426950408889634
MLA_TQ, MLA_TK = 256, 512
SB_TQ = 256
SB_UNDERFLOW = -104.0

A_WIDTH = 3 * 256
C_AGATE = 0
C_BCQ, C_BCKV, C_BKPE, C_BKPEROT = 256, 512, 640, 768
C_CG, C_BGATE, C_CQ, C_CSEL, C_CWIN, C_CGATE = 896, 1024, 1280, 1536, 1664, 1792
C_DQ, C_DK, C_DV, C_DGATE = 2048, 2304, 2560, 2816
C_WIDTH = 3072
W_C = A_WIDTH
W_KVC = A_WIDTH + C_WIDTH
W_WIDTH = W_KVC + 128
NSA_TK = 1024
ROPE_LANE = 64


def _dot(a, b):
    return jnp.dot(a, b, preferred_element_type=F32)


def _dot_nt(a, b):
    return lax.dot_general(a, b, (((1,), (1,)), ((), ())), preferred_element_type=F32)


def _cparams(sem):
    return pltpu.CompilerParams(dimension_semantics=sem, vmem_limit_bytes=VMEM_LIMIT)


def _const_spec(shape):
    nd = len(shape)
    return pl.BlockSpec(shape, lambda *_: (0,) * nd)


def _inproj_kernel(x_ref, g_ref, w_ref, a1_ref, a4_ref, a16_ref, c_ref, kvc_ref, a_scr):
    x = x_ref[...]
    ms = jnp.mean(x * x, axis=-1, keepdims=True)
    h = (x * lax.rsqrt(ms + RMS_EPS) * g_ref[...]).astype(BF16)
    cw = 256
    for j in range(A_WIDTH // cw):
        y = _dot(h, w_ref[:, j * cw:(j + 1) * cw])
        a1_ref[:, j * cw:(j + 1) * cw] = y.astype(BF16)
        for jj in range(cw // LANE):
            a_scr[(j * cw) // LANE + jj] = y[:, jj * LANE:(jj + 1) * LANE]
    tm = a_scr.shape[1]
    for dil, ref in ((4, a4_ref), (16, a16_ref)):
        for r in range(dil):
            for jl in range(A_WIDTH // LANE):
                rows = a_scr[jl, pl.ds(r, tm // dil, stride=dil), :]
                ref[:, r * A_WIDTH + jl * LANE:r * A_WIDTH + (jl + 1) * LANE] = rows.astype(BF16)
    for j in range(C_WIDTH // cw):
        c_ref[:, j * cw:(j + 1) * cw] = _dot(h, w_ref[:, W_C + j * cw:W_C + (j + 1) * cw]).astype(BF16)
    kvc_ref[...] = _dot(h, w_ref[:, W_KVC:W_KVC + 128]).astype(BF16)


def _inproj(x2, g, w):
    n, d = x2.shape
    tm = 512
    return pl.pallas_call(
        _inproj_kernel,
        grid=(n // tm,),
        in_specs=[pl.BlockSpec((tm, d), lambda i: (i, 0)),
                  _const_spec((1, d)),
                  _const_spec((d, W_WIDTH))],
        out_specs=[pl.BlockSpec((tm // dil, dil * A_WIDTH), lambda i: (i, 0)) for dil in DSW_DILATIONS]
        + [pl.BlockSpec((tm, C_WIDTH), lambda i: (i, 0)),
           pl.BlockSpec((tm, 128), lambda i: (i, 0))],
        out_shape=[jax.ShapeDtypeStruct((n // dil, dil * A_WIDTH), BF16) for dil in DSW_DILATIONS]
        + [jax.ShapeDtypeStruct((n, C_WIDTH), BF16),
           jax.ShapeDtypeStruct((n, 128), BF16)],
        scratch_shapes=[pltpu.VMEM((A_WIDTH // LANE, tm, LANE), F32)],
        compiler_params=_cparams(("arbitrary",)),
        name="inproj",
    )(x2, g, w)


def _dsw_kernel(q_ref, k_ref, v_ref, o_ref, lse_ref, *, dil, lq):
    blk = pl.program_id(2)
    row = lax.broadcasted_iota(jnp.int32, (TILE, 2 * TILE), 0)
    col = lax.broadcasted_iota(jnp.int32, (TILE, 2 * TILE), 1)
    for t in range(lq // TILE):
        u0 = blk * lq + t * TILE
        kstart = pl.multiple_of(jnp.maximum(u0 - TILE, 0), TILE)
        d = (u0 - kstart) + row - col
        valid = (d >= 0) & (d <= DSW_SPAN)
        df = d.astype(F32)
        q = q_ref[0, t * TILE:(t + 1) * TILE, :]
        kk = k_ref[0, pl.ds(kstart, 2 * TILE), :]
        vv = v_ref[0, pl.ds(kstart, 2 * TILE), :]
        hss = [slice(h * HEAD_DIM, (h + 1) * HEAD_DIM) for h in range(N_HEADS)]
        ss = [_dot_nt(q[:, hs], kk[:, hs]) for hs in hss]
        ps, ls, lses = [], [], []
        for h in range(N_HEADS):
            s = jnp.where(valid, ss[h] - (SLOPES_A[h] * dil) * df, NEG)
            m = jnp.max(s, axis=-1, keepdims=True)
            p = jnp.exp(s - m)
            l = jnp.sum(p, axis=-1, keepdims=True)
            ps.append(p.astype(BF16))
            ls.append(l)
            lses.append(jnp.broadcast_to(m + jnp.log(l), (TILE, HEAD_DIM)))
        outs = [_dot(ps[h], vv[:, hss[h]]) / ls[h] for h in range(N_HEADS)]
        o_ref[0, t * TILE:(t + 1) * TILE, :] = jnp.concatenate(outs, axis=-1).astype(BF16)
        lse_ref[0, t * TILE:(t + 1) * TILE, :] = jnp.concatenate(lses, axis=-1)


def _dsw_attention(cv, dil):
    b, l, _ = cv.shape
    lq = min(512, l)
    ncb = A_WIDTH // BRANCH_WIDTH
    qspec = pl.BlockSpec((1, lq, BRANCH_WIDTH), lambda bi, r, i: (bi, i, r * ncb))
    kspec = pl.BlockSpec((1, l, BRANCH_WIDTH), lambda bi, r, i: (bi, 0, r * ncb + 1))
    vspec = pl.BlockSpec((1, l, BRANCH_WIDTH), lambda bi, r, i: (bi, 0, r * ncb + 2))
    ospec = pl.BlockSpec((1, lq, BRANCH_WIDTH), lambda bi, r, i: (bi, i, r))
    o, lse = pl.pallas_call(
        functools.partial(_dsw_kernel, dil=dil, lq=lq),
        grid=(b, dil, l // lq),
        in_specs=[qspec, kspec, vspec],
        out_specs=[ospec, ospec],
        out_shape=[jax.ShapeDtypeStruct((b, l, dil * BRANCH_WIDTH), BF16),
                   jax.ShapeDtypeStruct((b, l, dil * BRANCH_WIDTH), F32)],
        compiler_params=_cparams(("arbitrary", "arbitrary", "arbitrary")),
        name=f"dsw{dil}",
    )(cv, cv, cv)
    return o.reshape(b * l, dil * BRANCH_WIDTH), lse.reshape(b * l, dil * BRANCH_WIDTH)


def _mla_prep_kernel(cq_ref, ckv_ref, kpe_ref, kper_ref, cos_ref, sin_ref, gq_ref, wl_ref, wr_ref,
                     gkv_ref, wk_ref, wv_ref, q_ref, k_ref, v_ref):
    cos = cos_ref[...]
    sin = sin_ref[...]
    cos4 = jnp.concatenate([cos] * N_HEADS, axis=-1)
    sin4 = jnp.concatenate([sin] * N_HEADS, axis=-1)
    cq = cq_ref[...].astype(F32)
    ms = jnp.sum(cq * cq, axis=-1, keepdims=True) * (1.0 / MLA_Q_RANK)
    qn = (cq * lax.rsqrt(ms + RMS_EPS) * gq_ref[...]).astype(BF16)
    scale = (MLA_NOPE_DIM + MLA_ROPE_DIM) ** -0.5 * LOG2E
    q = (_dot(qn, wl_ref[...]) * cos4 + _dot(qn, wr_ref[...]) * sin4) * scale
    q_ref[...] = q.astype(BF16)
    ckv = ckv_ref[...].astype(F32)
    ms = jnp.mean(ckv * ckv, axis=-1, keepdims=True)
    kn = (ckv * lax.rsqrt(ms + RMS_EPS) * gkv_ref[...]).astype(BF16)
    kpe = kpe_ref[...].astype(F32) * cos + kper_ref[...].astype(F32) * sin
    k = _dot(kn, wk_ref[...]) + jnp.concatenate([kpe] * N_HEADS, axis=-1)
    k_ref[...] = k.astype(BF16)
    lane = lax.broadcasted_iota(jnp.int32, (1, N_HEADS * LANE), 1) & (LANE - 1)
    v_ref[...] = (_dot(kn, wv_ref[...]) + jnp.where(lane == HEAD_DIM, 1.0, 0.0)).astype(BF16)


def _mla_prep(c2, s, cos_t, sin_t, gq, wl, wr, gkv, wk, wv):
    n = c2.shape[0]
    tm = 512
    spb = s // tm
    hq = N_HEADS * LANE

    def cspec(off, width):
        return pl.BlockSpec((tm, width), lambda i: (i, off // width))

    tspec = pl.BlockSpec((tm, LANE), lambda i: (i % spb, 0))
    return pl.pallas_call(
        _mla_prep_kernel,
        grid=(n // tm,),
        in_specs=[cspec(C_BCQ, 256), cspec(C_BCKV, 128), cspec(C_BKPE, 128), cspec(C_BKPEROT, 128),
                  tspec, tspec,
                  _const_spec((1, 256)), _const_spec((256, hq)), _const_spec((256, hq)),
                  _const_spec((1, 128)), _const_spec((128, hq)), _const_spec((128, hq))],
        out_specs=[pl.BlockSpec((tm, hq), lambda i: (i, 0)),
                   pl.BlockSpec((tm, hq), lambda i: (i, 0)),
                   pl.BlockSpec((tm, hq), lambda i: (i, 0))],
        out_shape=[jax.ShapeDtypeStruct((n, hq), BF16),
                   jax.ShapeDtypeStruct((n, hq), BF16),
                   jax.ShapeDtypeStruct((n, hq), BF16)],
        compiler_params=_cparams(("arbitrary",)),
        name="mla_prep",
    )(c2, c2, c2, c2, cos_t, sin_t, gq, wl, wr, gkv, wk, wv)


def _mla_flash_kernel(q_ref, k_ref, v_ref, o_ref):
    i = pl.program_id(1)
    t0 = i * MLA_TQ
    row = lax.broadcasted_iota(jnp.int32, (MLA_TQ, MLA_TK), 0)
    col = lax.broadcasted_iota(jnp.int32, (MLA_TQ, MLA_TK), 1)

    def step(j, carry, masked):
        ks = pl.multiple_of(j * MLA_TK, MLA_TK)
        if masked:
            keep = (col - row) <= (t0 - ks)
        hls = [slice(h * LANE, (h + 1) * LANE) for h in range(N_HEADS)]
        ss = [_dot_nt(q_ref[0, :, hl], k_ref[0, pl.ds(ks, MLA_TK), hl]) for hl in hls]
        ps, ms = [], []
        for h in range(N_HEADS):
            s = jnp.where(keep, ss[h], NEG) if masked else ss[h]
            m_new = jnp.maximum(carry[h][0], jnp.max(s, axis=-1, keepdims=True))
            ps.append(jnp.exp2(s - m_new).astype(BF16))
            ms.append(m_new)
        new = []
        for h in range(N_HEADS):
            m, acc = carry[h]
            acc = jnp.exp2(m - ms[h]) * acc + _dot(ps[h], v_ref[0, pl.ds(ks, MLA_TK), hls[h]])
            new.append((ms[h], acc))
        return tuple(new)

    init = tuple((jnp.full((MLA_TQ, 1), NEG, F32), jnp.zeros((MLA_TQ, LANE), F32)) for _ in range(N_HEADS))
    nfull = t0 // MLA_TK
    carry = lax.fori_loop(0, nfull, functools.partial(step, masked=False), init)
    carry = step(nfull, carry, True)
    o_ref[0] = jnp.concatenate([acc[:, 0:HEAD_DIM] / acc[:, HEAD_DIM:HEAD_DIM + 1] for (_, acc) in carry],
                               axis=-1).astype(BF16)


def _mla_flash(q3, k3, v3):
    b, s, hq = q3.shape
    return pl.pallas_call(
        _mla_flash_kernel,
        grid=(b, s // MLA_TQ),
        in_specs=[pl.BlockSpec((1, MLA_TQ, hq), lambda bi, i: (bi, i, 0)),
                  pl.BlockSpec((1, s, hq), lambda bi, i: (bi, 0, 0)),
                  pl.BlockSpec((1, s, hq), lambda bi, i: (bi, 0, 0))],
        out_specs=pl.BlockSpec((1, MLA_TQ, BRANCH_WIDTH), lambda bi, i: (bi, i, 0)),
        out_shape=jax.ShapeDtypeStruct((b, s, BRANCH_WIDTH), BF16),
        compiler_params=_cparams(("arbitrary", "arbitrary")),
        name="mla_flash",
    )(q3, k3, v3)


def _nsa_cmp_kernel(ch_ref, pt_ref, pb_ref, wt_ref, wb_ref, b1_ref, w2_ref, b2_ref, o_ref):
    ch = ch_ref[0]
    top = _dot(ch, wt_ref[...])
    bot = _dot(ch, wb_ref[...])
    const = (_dot(pt_ref[...], wt_ref[...]) + _dot(pb_ref[...], wb_ref[...]))[0:1, :] + b1_ref[...]
    n = ch.shape[0]
    hid = top + pltpu.roll(bot, n - 1, 0) + const
    hid = jax.nn.gelu(hid)
    o_ref[0] = (_dot(hid.astype(BF16), w2_ref[...]) + b2_ref[...]).astype(BF16)


def _nsa_compress(kvc3, pos_t, pos_b, wt, wb, b1, w2, b2):
    b, s, _ = kvc3.shape
    nch = s // NSA_CMP_STRIDE
    cw = NSA_CMP_STRIDE * 128
    ch = kvc3.reshape(b, nch, cw)
    return pl.pallas_call(
        _nsa_cmp_kernel,
        grid=(b,),
        in_specs=[pl.BlockSpec((1, nch, cw), lambda bi: (bi, 0, 0)),
                  _const_spec((8, cw)), _const_spec((8, cw)),
                  _const_spec((cw, 2 * NSA_CMP_HIDDEN)), _const_spec((cw, 2 * NSA_CMP_HIDDEN)),
                  _const_spec((1, 2 * NSA_CMP_HIDDEN)),
                  _const_spec((2 * NSA_CMP_HIDDEN, 2 * NSA_KV_DIM)), _const_spec((1, 2 * NSA_KV_DIM))],
        out_specs=pl.BlockSpec((1, nch, 2 * NSA_KV_DIM), lambda bi: (bi, 0, 0)),
        out_shape=jax.ShapeDtypeStruct((b, nch, 2 * NSA_KV_DIM), BF16),
        compiler_params=_cparams(("arbitrary",)),
        name="nsa_compress",
    )(ch, pos_t, pos_b, wt, wb, b1, w2, b2)


def _split_dot_nt(w, x, terms=3):
    out = None
    rem = x
    for _ in range(terms):
        part = rem.astype(BF16)
        rem = rem - part.astype(F32)
        y = _dot_nt(w, part)
        out = y if out is None else out + y
    return out


def _nsa_sel_table(s):
    t = np.arange(s)
    tab = np.zeros((s, 2 * LANE), np.float32)
    tab[:, HEAD_DIM] = t >> 7
    tab[:, HEAD_DIM + 1] = t & 127
    tab[t, LANE + t // NSA_SEL_BLOCK] = NEG
    return jnp.asarray(tab, dtype=BF16)


def _nsa_cmp_table(ncmp):
    end = np.arange(ncmp) * NSA_CMP_STRIDE + NSA_CMP_BLOCK - 1
    tab = np.zeros((ncmp, LANE), np.float32)
    tab[:, HEAD_DIM] = end >> 7
    tab[:, HEAD_DIM + 1] = end & 127
    return jnp.asarray(tab, dtype=BF16)


def _nsa_kernel(q_ref, g_ref, cmp_ref, sel_ref, win_ref, tab_ref, ctab_ref, o_ref, *, ncmp):
    i = pl.program_id(1)
    t0 = i * TILE
    hr = N_HEADS * TILE
    q = q_ref[0]
    q4 = jnp.concatenate([q[:, h * HEAD_DIM:(h + 1) * HEAD_DIM] for h in range(N_HEADS)], axis=0)
    rowh = lax.broadcasted_iota(jnp.int32, (hr, 1), 0)
    hidx = rowh >> 7
    r = rowh & (TILE - 1)
    slope = jnp.where(hidx == 0, SLOPES_C[0],
                      jnp.where(hidx == 1, SLOPES_C[1], jnp.where(hidx == 2, SLOPES_C[2], SLOPES_C[3]))).astype(F32)

    lane64 = lax.broadcasted_iota(jnp.int32, (hr, HEAD_DIM), 1)
    alibi_q = jnp.where(lane64 == 0, slope * 128.0, jnp.where(lane64 == 1, slope, 0.0)).astype(BF16)
    q4a = jnp.concatenate([q4, alibi_q], axis=1)
    lane = lax.broadcasted_iota(jnp.int32, (1, LANE), 1)
    keep_k = jnp.where(lane < NSA_KV_DIM, 1.0, 0.0).astype(BF16)
    keep_v = jnp.where(lane >= NSA_KV_DIM, 1.0, 0.0).astype(BF16)
    one0 = jnp.where(lane == 0, 1.0, 0.0).astype(BF16)

    def value_operand(kv):
        return kv * keep_v + one0

    def masked_softmax(s, mask):
        sb = jnp.where(mask, s, NEG)
        m = jnp.maximum(jnp.max(sb, axis=-1, keepdims=True), 0.1 * NEG)
        return jnp.exp(sb - m)

    def normalise(pv):
        return pv[:, NSA_KV_DIM:2 * NSA_KV_DIM], jnp.maximum(pv[:, 0:1], 1e-30)

    ckv = cmp_ref[0]
    ncol = lax.broadcasted_iota(jnp.int32, (1, ncmp), 1)
    last_ok = (t0 + r - (NSA_CMP_BLOCK - 1)) >> 4
    s_cmp = _dot_nt(q4a, ckv * keep_k + ctab_ref[...])
    wlen = NSA_WINDOW + TILE
    kstart = pl.multiple_of(jnp.maximum(t0 - NSA_WINDOW, 0), TILE)
    wkv = win_ref[0, pl.ds(kstart, wlen), :]
    s_win = _dot_nt(q4a, wkv * keep_k + tab_ref[pl.ds(kstart, wlen), 0:LANE])
    e_cmp = masked_softmax(s_cmp, ncol <= last_ok)
    den_c = jnp.maximum(jnp.sum(e_cmp, axis=-1, keepdims=True), 1e-30)
    p_cmp = e_cmp / den_c
    o_cmp = _dot(p_cmp.astype(BF16), ckv)[:, NSA_KV_DIM:2 * NSA_KV_DIM]
    psum = p_cmp[0:TILE] + p_cmp[TILE:2 * TILE] + p_cmp[2 * TILE:3 * TILE] + p_cmp[3 * TILE:4 * TILE]
    jrow = lax.broadcasted_iota(jnp.int32, (TILE, ncmp), 0)
    ncol2 = lax.broadcasted_iota(jnp.int32, (TILE, ncmp), 1)
    delta = jrow * (NSA_SEL_BLOCK // NSA_CMP_STRIDE) - ncol2
    selmap = jnp.where((delta == 0) | (delta == 4), 1.0,
                       jnp.where((delta > 0) & (delta < 4), 2.0, 0.0)).astype(BF16)
    p_sel = _split_dot_nt(selmap, psum)

    blk = lax.broadcasted_iota(jnp.int32, (TILE, TILE), 0)
    tq = t0 + lax.broadcasted_iota(jnp.int32, (TILE, TILE), 1)
    cur = tq >> 6
    forced = (blk == 0) | (blk == cur) | (blk == cur - 1)
    cand = jnp.where((blk <= cur) & jnp.logical_not(forced), p_sel, -1.0)
    picked = jnp.where(forced, 1.0, 0.0)
    for _ in range(NSA_TOP_N - 3):
        m = jnp.max(cand, axis=0, keepdims=True)
        idx = jnp.min(jnp.where(cand == m, blk, TILE), axis=0, keepdims=True)
        hit = (blk == idx) & (m >= 0.0)
        picked = jnp.where(hit, 1.0, picked)
        cand = jnp.where(hit, -1.0, cand)
    not_sel = jnp.transpose(1.0 - picked).astype(BF16)

    q4aug = jnp.concatenate([q4a, jnp.concatenate([not_sel] * N_HEADS, axis=0)], axis=1)
    colk = lax.broadcasted_iota(jnp.int32, (hr, NSA_TK), 1)

    def sel_step(j, carry, diag):
        m, acc = carry
        ks = pl.multiple_of(j * NSA_TK, NSA_TK)
        kv = sel_ref[0, pl.ds(ks, NSA_TK), :]
        tb = tab_ref[pl.ds(ks, NSA_TK), :]
        kaug = jnp.concatenate([kv * keep_k + tb[:, 0:LANE], tb[:, LANE:2 * LANE]], axis=1)
        s = _dot_nt(q4aug, kaug)
        if diag:
            s = jnp.where((colk - r) <= (t0 - ks), s, NEG)
        m_new = jnp.maximum(m, jnp.max(s, axis=-1, keepdims=True))
        p = jnp.exp(s - m_new)
        acc = jnp.exp(m - m_new) * acc + _dot(p.astype(BF16), value_operand(kv))
        return m_new, acc

    init = (jnp.full((hr, 1), NEG, F32), jnp.zeros((hr, LANE), F32))
    jd = t0 // NSA_TK
    carry = sel_step(jd, init, True)
    _, acc = lax.fori_loop(0, jd, functools.partial(sel_step, diag=False), carry)
    o_slc, den_s = normalise(acc)
    o_slc = o_slc / den_s

    colw = lax.broadcasted_iota(jnp.int32, (hr, wlen), 1)
    dist_w = (t0 - kstart) + r - colw
    e_win = masked_softmax(s_win, (dist_w >= 0) & (dist_w < NSA_WINDOW))
    o_win, den_w = normalise(_dot(e_win.astype(BF16), value_operand(wkv)))
    o_win = o_win / den_w

    g = jax.nn.sigmoid(g_ref[0].astype(F32))

    def gate_col(kk):
        return jnp.concatenate([g[:, 3 * h + kk:3 * h + kk + 1] for h in range(N_HEADS)], axis=0)

    out4 = gate_col(0) * o_cmp + gate_col(1) * o_slc + gate_col(2) * o_win
    o_ref[0] = jnp.concatenate([out4[h * TILE:(h + 1) * TILE] for h in range(N_HEADS)], axis=-1).astype(BF16)


def _nsa_attention(c3, cmp3):
    b, s, _ = c3.shape
    ncmp = cmp3.shape[1]
    return pl.pallas_call(
        functools.partial(_nsa_kernel, ncmp=ncmp),
        grid=(b, s // TILE),
        in_specs=[pl.BlockSpec((1, TILE, BRANCH_WIDTH), lambda bi, i: (bi, i, C_CQ // BRANCH_WIDTH)),
                  pl.BlockSpec((1, TILE, 128), lambda bi, i: (bi, i, C_CG // 128)),
                  pl.BlockSpec((1, ncmp, 128), lambda bi, i: (bi, 0, 0)),
                  pl.BlockSpec((1, s, 128), lambda bi, i: (bi, 0, C_CSEL // 128)),
                  pl.BlockSpec((1, s, 128), lambda bi, i: (bi, 0, C_CWIN // 128)),
                  _const_spec((s, 2 * LANE)), _const_spec((ncmp, LANE))],
        out_specs=pl.BlockSpec((1, TILE, BRANCH_WIDTH), lambda bi, i: (bi, i, 0)),
        out_shape=jax.ShapeDtypeStruct((b, s, BRANCH_WIDTH), BF16),
        compiler_params=_cparams(("arbitrary", "arbitrary")),
        name="nsa_attn",
    )(c3, c3, cmp3, c3, c3, _nsa_sel_table(s), _nsa_cmp_table(ncmp))


def _sb_kernel(q_ref, k_ref, v_ref, o_ref, run_ref, acc_ref):
    i = pl.program_id(1)
    t0 = i * SB_TQ
    row = lax.broadcasted_iota(jnp.int32, (SB_TQ, TILE), 0)
    col = lax.broadcasted_iota(jnp.int32, (SB_TQ, TILE), 1)
    urow = lax.broadcasted_iota(jnp.int32, (2 * TILE, 2 * TILE), 0) & (TILE - 1)
    ucol = lax.broadcasted_iota(jnp.int32, (2 * TILE, 2 * TILE), 1)
    umat = jnp.where((ucol >= TILE) | (urow > ucol), 1.0, 0.0).astype(BF16)
    run_ref[...] = jnp.zeros_like(run_ref)
    acc_ref[...] = jnp.zeros_like(acc_ref)

    def tile(j, masked):
        ks = pl.multiple_of(j * TILE, TILE)
        if masked:
            strict = (col - row) < (t0 - ks)
        kt = k_ref[0, pl.ds(ks, TILE), :]
        vt = v_ref[0, pl.ds(ks, TILE), :]
        hss = [slice(h * HEAD_DIM, (h + 1) * HEAD_DIM) for h in range(N_HEADS)]
        zs = [_dot_nt(q_ref[0, :, hs], kt[:, hs]) for hs in hss]
        lss, css = [], []
        for z in zs:
            ls = jnp.minimum(z, 0.0) - jnp.log(1.0 + jnp.exp(-jnp.abs(z)))
            l1m = ls - z
            if masked:
                l1m = jnp.where(strict, l1m, 0.0)
            hi = l1m.astype(BF16)
            lo = (l1m - hi.astype(F32)).astype(BF16)
            lss.append(ls)
            css.append(_dot(jnp.concatenate([hi, lo], axis=-1), umat))
        for h in range(N_HEADS):
            run = run_ref[h]
            a = jnp.exp(lss[h] + run + css[h][:, 0:TILE])
            if masked:
                a = jnp.where(strict, a, 0.0)
            acc_ref[h] += _dot(a.astype(BF16), vt[:, hss[h]])
            run_ref[h] = run + css[h][:, TILE:2 * TILE]

    def run_max():
        return jnp.max(jnp.maximum(jnp.maximum(run_ref[0], run_ref[1]), jnp.maximum(run_ref[2], run_ref[3])))

    j_top = t0 // TILE + SB_TQ // TILE - 1
    for dj in range(SB_TQ // TILE):
        tile(j_top - dj, True)

    def cond(c):
        return (c[0] >= 0) & (c[1] > SB_UNDERFLOW)

    def body(c):
        tile(c[0], False)
        return c[0] - 1, run_max()

    lax.while_loop(cond, body, (j_top - SB_TQ // TILE, run_max()))
    o_ref[0] = jnp.concatenate([acc_ref[h] for h in range(N_HEADS)], axis=-1).astype(BF16)


def _sb_attention(c3):
    b, s, _ = c3.shape
    return pl.pallas_call(
        _sb_kernel,
        grid=(b, s // SB_TQ),
        in_specs=[pl.BlockSpec((1, SB_TQ, BRANCH_WIDTH), lambda bi, i: (bi, i, C_DQ // BRANCH_WIDTH)),
                  pl.BlockSpec((1, s, BRANCH_WIDTH), lambda bi, i: (bi, 0, C_DK // BRANCH_WIDTH)),
                  pl.BlockSpec((1, s, BRANCH_WIDTH), lambda bi, i: (bi, 0, C_DV // BRANCH_WIDTH))],
        out_specs=pl.BlockSpec((1, SB_TQ, BRANCH_WIDTH), lambda bi, i: (bi, i, 0)),
        out_shape=jax.ShapeDtypeStruct((b, s, BRANCH_WIDTH), BF16),
        scratch_shapes=[pltpu.VMEM((N_HEADS, SB_TQ, TILE), F32), pltpu.VMEM((N_HEADS, SB_TQ, HEAD_DIM), F32)],
        compiler_params=_cparams(("arbitrary", "arbitrary")),
        name="stickbreak",
    )(c3, c3, c3)


def _merge_kernel(x_ref, ng_ref, o1_ref, o2_ref, o3_ref, l1_ref, l2_ref, l3_ref, ga_ref,
                  yb_ref, gb_ref, yc_ref, gc_ref, yd_ref, gd_ref,
                  wm_ref, bm_ref, wb_ref, wo_ref, fg_ref, out_ref, *scratch, final):
    x = x_ref[...]
    ms = jnp.mean(x * x, axis=-1, keepdims=True)
    h = (x * lax.rsqrt(ms + RMS_EPS) * ng_ref[...]).astype(BF16)

    def token_order(ref, scr, dil):
        rows = x.shape[0] // dil
        for r in range(dil):
            for jl in range(BRANCH_WIDTH // LANE):
                lo = r * BRANCH_WIDTH + jl * LANE
                scr[jl, pl.ds(r, rows, stride=dil), :] = ref[:, lo:lo + LANE].astype(F32)
        return jnp.concatenate([scr[jl] for jl in range(BRANCH_WIDTH // LANE)], axis=-1)

    o2, l2 = token_order(o2_ref, scratch[0], 4), token_order(l2_ref, scratch[1], 4)
    o3, l3 = token_order(o3_ref, scratch[2], 16), token_order(l3_ref, scratch[3], 16)
    l1 = l1_ref[...]
    lm = jnp.maximum(jnp.maximum(l1, l2), l3)
    e1, e2, e3 = jnp.exp(l1 - lm), jnp.exp(l2 - lm), jnp.exp(l3 - lm)
    ya = (e1 * o1_ref[...].astype(F32) + e2 * o2 + e3 * o3) / (e1 + e2 + e3)
    ys = (ya, yb_ref[...].astype(F32), yc_ref[...].astype(F32), yd_ref[...].astype(F32))
    gs = (ga_ref, gb_ref, gc_ref, gd_ref)
    merged = None
    for i in range(4):
        y = (ys[i] * jax.nn.silu(gs[i][...].astype(F32))).astype(BF16)
        gate = jax.nn.sigmoid(_dot(h, wm_ref[i]) + bm_ref[i])
        term = gate * _dot(y, wb_ref[i])
        merged = term if merged is None else merged + term
    xn = x + _dot(merged.astype(BF16), wo_ref[...])
    if final:
        ms = jnp.mean(xn * xn, axis=-1, keepdims=True)
        xn = xn * lax.rsqrt(ms + RMS_EPS) * fg_ref[...]
    out_ref[...] = xn


def _merge(x2, ng, a_outs, a_lses, c2, yb, yc, yd, wm, bm, wb, wo, fg, final):
    n, d = x2.shape
    tm = 256
    bw = BRANCH_WIDTH

    def rows(width):
        return pl.BlockSpec((tm, width), lambda i: (i, 0))

    def cspec(off):
        return pl.BlockSpec((tm, bw), lambda i: (i, off // bw))

    def dspec(dil):
        return pl.BlockSpec((tm // dil, dil * bw), lambda i: (i, 0))

    return pl.pallas_call(
        functools.partial(_merge_kernel, final=final),
        grid=(n // tm,),
        in_specs=[rows(d), _const_spec((1, d)),
                  dspec(1), dspec(4), dspec(16), dspec(1), dspec(4), dspec(16), cspec(C_AGATE),
                  rows(bw), cspec(C_BGATE), rows(bw), cspec(C_CGATE), rows(bw), cspec(C_DGATE),
                  _const_spec((4, d, d)), _const_spec((4, 1, d)), _const_spec((4, bw, d)),
                  _const_spec((d, d)), _const_spec((1, d))],
        out_specs=rows(d),
        out_shape=jax.ShapeDtypeStruct((n, d), F32),
        scratch_shapes=[pltpu.VMEM((bw // LANE, tm, LANE), F32) for _ in range(4)],
        compiler_params=_cparams(("arbitrary",)),
        name="merge",
    )(x2, ng, *a_outs, *a_lses, c2, yb, c2, yc, c2, yd, c2, wm, bm, wb, wo, fg)


def _split_w_in(w):
    widths = (256, 256, 256, 256, MLA_Q_RANK, MLA_KV_RANK, MLA_ROPE_DIM, 256,
              256, 64, 64, 64, 64, 64, 64, 3 * N_HEADS, 256, 256, 256, 256, 256)
    names = ("a_q", "a_k", "a_v", "a_gate", "b_cq", "b_ckv", "b_kpe", "b_gate",
             "c_q", "c_kc", "c_vc", "c_ks", "c_vs", "c_kw", "c_vw", "c_g", "c_gate",
             "d_q", "d_k", "d_v", "d_gate")
    out, off = {}, 0
    for nm, wd in zip(names, widths):
        out[nm] = w[:, off:off + wd]
        off += wd
    return out


def _rot_cols(w):
    half = w.shape[1] // 2
    return jnp.concatenate([-w[:, half:], w[:, :half]], axis=1)


def _prep_w_in(w):
    p = _split_w_in(w)
    d = w.shape[0]
    qs = HEAD_DIM ** -0.5

    def z(n):
        return jnp.zeros((d, n), w.dtype)

    kpe_blk = jnp.concatenate([z(ROPE_LANE), p["b_kpe"], z(LANE - ROPE_LANE - MLA_ROPE_DIM)], axis=1)
    kper_blk = jnp.concatenate([z(ROPE_LANE), _rot_cols(p["b_kpe"]), z(LANE - ROPE_LANE - MLA_ROPE_DIM)], axis=1)
    cols = [p["a_q"] * qs, p["a_k"], p["a_v"],
            p["a_gate"], p["b_cq"], z(256 - MLA_Q_RANK), p["b_ckv"], kpe_blk, kper_blk,
            p["c_g"], z(128 - 3 * N_HEADS), p["b_gate"],
            p["c_q"] * qs, p["c_ks"], p["c_vs"], p["c_kw"], p["c_vw"], p["c_gate"],
            p["d_q"] * qs, p["d_k"], p["d_v"], p["d_gate"],
            p["c_kc"], p["c_vc"]]
    out = jnp.concatenate(cols, axis=1)
    assert out.shape[1] == W_WIDTH
    return out.astype(BF16)


def _prep_mla(w_uq, w_ukv, gq):
    qd = MLA_NOPE_DIM + MLA_ROPE_DIM
    pad = LANE - qd
    wl, wr, wk, wv = [], [], [], []
    for h in range(N_HEADS):
        nope = w_uq[:, h * qd:h * qd + MLA_NOPE_DIM]
        rp = w_uq[:, h * qd + MLA_NOPE_DIM:(h + 1) * qd]
        zq = jnp.zeros((MLA_Q_RANK, pad), w_uq.dtype)
        wl += [nope, rp, zq]
        wr += [jnp.zeros_like(nope), _rot_cols(rp), zq]
        kn = w_ukv[:, h * 128:h * 128 + MLA_NOPE_DIM]
        wk += [kn, jnp.zeros((MLA_KV_RANK, LANE - MLA_NOPE_DIM), w_ukv.dtype)]
        wv += [w_ukv[:, h * 128 + MLA_NOPE_DIM:(h + 1) * 128],
               jnp.zeros((MLA_KV_RANK, LANE - HEAD_DIM), w_ukv.dtype)]
    rpad = ((0, 256 - MLA_Q_RANK), (0, 0))
    wl = jnp.pad(jnp.concatenate(wl, axis=1), rpad).astype(BF16)
    wr = jnp.pad(jnp.concatenate(wr, axis=1), rpad).astype(BF16)
    gq = jnp.pad(gq, (0, 256 - MLA_Q_RANK)).reshape(1, 256)
    return wl, wr, jnp.concatenate(wk, axis=1).astype(BF16), jnp.concatenate(wv, axis=1).astype(BF16), gq


def _rope_tables(s):
    half = MLA_ROPE_DIM // 2
    inv = ROPE_THETA ** (-jnp.arange(half, dtype=F32) / half)
    ang = jnp.arange(s, dtype=F32)[:, None] * inv[None, :]
    cos, sin = jnp.cos(ang), jnp.sin(ang)
    ones = jnp.ones((s, ROPE_LANE), F32)
    zl = jnp.zeros((s, ROPE_LANE), F32)
    zr = jnp.zeros((s, LANE - ROPE_LANE - MLA_ROPE_DIM), F32)
    return (jnp.concatenate([ones, cos, cos, zr], axis=1), jnp.concatenate([zl, sin, sin, zr], axis=1))


def _prep_nsa(pos, w1, b1, w2, b2):
    half = NSA_CMP_BLOCK // 2
    eye = jnp.eye(2, dtype=w1.dtype)
    w1r = w1.reshape(2, NSA_CMP_BLOCK, NSA_KV_DIM, NSA_CMP_HIDDEN)
    posr = pos

    def comb_w(part):
        return jnp.einsum("ktdh,kj->tkdjh", part, eye).reshape(half * 2 * NSA_KV_DIM, 2 * NSA_CMP_HIDDEN)

    def comb_p(part):
        flat = jnp.transpose(part, (1, 0, 2)).reshape(1, half * 2 * NSA_KV_DIM)
        return jnp.pad(flat, ((0, 7), (0, 0))).astype(BF16)

    wt, wb = comb_w(w1r[:, :half]).astype(BF16), comb_w(w1r[:, half:]).astype(BF16)
    pt, pb = comb_p(posr[:, :half]), comb_p(posr[:, half:])
    b1c = b1.reshape(1, 2 * NSA_CMP_HIDDEN)
    zero = jnp.zeros((NSA_CMP_HIDDEN, NSA_KV_DIM), w2.dtype)
    w2c = jnp.concatenate([jnp.concatenate([w2[0], zero], axis=1),
                           jnp.concatenate([zero, w2[1]], axis=1)], axis=0).astype(BF16)
    b2c = b2.reshape(1, 2 * NSA_KV_DIM)
    return pt, pb, wt, wb, b1c, w2c, b2c


def kernel(x, norm_g, w_in, mla_q_norm, mla_w_uq, mla_kv_norm, mla_w_ukv, nsa_pos, nsa_w1, nsa_b1,
           nsa_w2, nsa_b2, w_branch, w_merge, b_merge, w_out, final_norm_g):
    b, s, d = x.shape
    depth = norm_g.shape[0]
    n = b * s
    assert s % (TILE * max(DSW_DILATIONS)) == 0 and s >= NSA_WINDOW + TILE
    cos_t, sin_t = _rope_tables(s)
    x2 = x.reshape(n, d)
    for layer in range(depth):
        *a_cls, c2, kvc = _inproj(x2, norm_g[layer].reshape(1, d), _prep_w_in(w_in[layer]))
        c3 = c2.reshape(b, s, C_WIDTH)
        a_res = [_dsw_attention(a.reshape(b, s // dil, dil * A_WIDTH), dil) for a, dil in zip(a_cls, DSW_DILATIONS)]
        wl, wr, wk, wv, gq = _prep_mla(mla_w_uq[layer], mla_w_ukv[layer], mla_q_norm[layer])
        qb, kb, vb = _mla_prep(c2, s, cos_t, sin_t, gq, wl, wr, mla_kv_norm[layer].reshape(1, MLA_KV_RANK), wk, wv)
        hq = N_HEADS * LANE
        yb = _mla_flash(qb.reshape(b, s, hq), kb.reshape(b, s, hq), vb.reshape(b, s, hq))
        cmp3 = _nsa_compress(kvc.reshape(b, s, 128),
                             *_prep_nsa(nsa_pos[layer], nsa_w1[layer], nsa_b1[layer], nsa_w2[layer], nsa_b2[layer]))
        yc = _nsa_attention(c3, cmp3)
        yd = _sb_attention(c3)
        x2 = _merge(x2, norm_g[layer].reshape(1, d), [r[0] for r in a_res], [r[1] for r in a_res], c2,
                    yb.reshape(n, BRANCH_WIDTH), yc.reshape(n, BRANCH_WIDTH), yd.reshape(n, BRANCH_WIDTH),
                    w_merge[layer].astype(BF16), b_merge[layer].reshape(4, 1, d),
                    w_branch[layer].astype(BF16), w_out[layer].astype(BF16),
                    final_norm_g.reshape(1, d), final=(layer == depth - 1))
    return x2.reshape(b, s, d)
```

```python
import functools

import numpy as np
import jax
import jax.numpy as jnp
from jax import lax
from jax.experimental import pallas as pl
from jax.experimental.pallas import tpu as pltpu

F32 = jnp.float32
BF16 = jnp.bfloat16

HEAD_DIM = 64
N_HEADS = 4
BRANCH_WIDTH = N_HEADS * HEAD_DIM
RMS_EPS = 1e-6
DSW_DILATIONS = (1, 4, 16)
DSW_SPAN = 128
MLA_Q_RANK = 192
MLA_KV_RANK = 128
MLA_NOPE_DIM = 64
MLA_ROPE_DIM = 32
ROPE_THETA = 10000.0
NSA_CMP_BLOCK = 32
NSA_CMP_STRIDE = 16
NSA_CMP_HIDDEN = 128
NSA_SEL_BLOCK = 64
NSA_TOP_N = 16
NSA_WINDOW = 512
NSA_KV_DIM = 64
_ALIBI = [2.0 ** (-(i + 1)) for i in range(8)]
SLOPES_A = _ALIBI[0::2]
SLOPES_C = _ALIBI[1::2]

LANE = 128
TILE = 128
VMEM_LIMIT = 56 * 1024 * 1024
NEG = -1e30
LOG2E = 1.4426950408889634
MLA_TQ, MLA_TK = 256, 1024
SB_TQ = 256
SB_UNDERFLOW = -104.0

A_WIDTH = 3 * 256
C_AGATE = 0
C_BCQ, C_BCKV, C_BKPE, C_BKPEROT = 256, 512, 640, 768
C_CG, C_BGATE, C_CQ, C_CSEL, C_CWIN, C_CGATE = 896, 1024, 1280, 1536, 1664, 1792
C_DQ, C_DK, C_DV, C_DGATE = 2048, 2304, 2560, 2816
C_WIDTH = 3072
W_C = A_WIDTH
W_KVC = A_WIDTH + C_WIDTH
W_WIDTH = W_KVC + 128
NSA_TK = 1024
ROPE_LANE = 64


def _dot(a, b):
    return jnp.dot(a, b, preferred_element_type=F32)


def _dot_nt(a, b):
    return lax.dot_general(a, b, (((1,), (1,)), ((), ())), preferred_element_type=F32)


def _cparams(sem):
    return pltpu.CompilerParams(dimension_semantics=sem, vmem_limit_bytes=VMEM_LIMIT)


def _const_spec(shape):
    nd = len(shape)
    return pl.BlockSpec(shape, lambda *_: (0,) * nd)


def _inproj_kernel(x_ref, g_ref, w_ref, a1_ref, a4_ref, a16_ref, c_ref, kvc_ref, a_scr):
    x = x_ref[...]
    ms = jnp.mean(x * x, axis=-1, keepdims=True)
    h = (x * lax.rsqrt(ms + RMS_EPS) * g_ref[...]).astype(BF16)
    cw = 256
    for j in range(A_WIDTH // cw):
        y = _dot(h, w_ref[:, j * cw:(j + 1) * cw])
        a1_ref[:, j * cw:(j + 1) * cw] = y.astype(BF16)
        for jj in range(cw // LANE):
            a_scr[(j * cw) // LANE + jj] = y[:, jj * LANE:(jj + 1) * LANE]
    tm = a_scr.shape[1]
    for dil, ref in ((4, a4_ref), (16, a16_ref)):
        for r in range(dil):
            for jl in range(A_WIDTH // LANE):
                rows = a_scr[jl, pl.ds(r, tm // dil, stride=dil), :]
                ref[:, r * A_WIDTH + jl * LANE:r * A_WIDTH + (jl + 1) * LANE] = rows.astype(BF16)
    for j in range(C_WIDTH // cw):
        c_ref[:, j * cw:(j + 1) * cw] = _dot(h, w_ref[:, W_C + j * cw:W_C + (j + 1) * cw]).astype(BF16)
    kvc_ref[...] = _dot(h, w_ref[:, W_KVC:W_KVC + 128]).astype(BF16)


def _inproj(x2, g, w):
    n, d = x2.shape
    tm = 512
    return pl.pallas_call(
        _inproj_kernel,
        grid=(n // tm,),
        in_specs=[pl.BlockSpec((tm, d), lambda i: (i, 0)),
                  _const_spec((1, d)),
                  _const_spec((d, W_WIDTH))],
        out_specs=[pl.BlockSpec((tm // dil, dil * A_WIDTH), lambda i: (i, 0)) for dil in DSW_DILATIONS]
        + [pl.BlockSpec((tm, C_WIDTH), lambda i: (i, 0)),
           pl.BlockSpec((tm, 128), lambda i: (i, 0))],
        out_shape=[jax.ShapeDtypeStruct((n // dil, dil * A_WIDTH), BF16) for dil in DSW_DILATIONS]
        + [jax.ShapeDtypeStruct((n, C_WIDTH), BF16),
           jax.ShapeDtypeStruct((n, 128), BF16)],
        scratch_shapes=[pltpu.VMEM((A_WIDTH // LANE, tm, LANE), F32)],
        compiler_params=_cparams(("arbitrary",)),
        name="inproj",
    )(x2, g, w)


def _dsw_kernel(q_ref, k_ref, v_ref, o_ref, lse_ref, *, dil, lq):
    blk = pl.program_id(2)
    row = lax.broadcasted_iota(jnp.int32, (TILE, 2 * TILE), 0)
    col = lax.broadcasted_iota(jnp.int32, (TILE, 2 * TILE), 1)
    hss = [slice(h * HEAD_DIM, (h + 1) * HEAD_DIM) for h in range(N_HEADS)]
    nt = lq // TILE

    def operands(t):
        u0 = blk * lq + t * TILE
        kstart = pl.multiple_of(jnp.maximum(u0 - TILE, 0), TILE)
        d = (u0 - kstart) + row - col
        valid = (d >= 0) & (d <= DSW_SPAN)
        return (q_ref[0, t * TILE:(t + 1) * TILE, :], k_ref[0, pl.ds(kstart, 2 * TILE), :],
                v_ref[0, pl.ds(kstart, 2 * TILE), :], valid, d.astype(F32))

    def softmax(s, h, valid, df):
        s = jnp.where(valid, s - (SLOPES_A[h] * dil) * df, NEG)
        m = jnp.max(s, axis=-1, keepdims=True)
        p = jnp.exp(s - m)
        l = jnp.sum(p, axis=-1, keepdims=True)
        return p.astype(BF16), l, jnp.broadcast_to(m + jnp.log(l), (TILE, HEAD_DIM))

    ops = [operands(t) for t in range(nt)]
    ss = [[_dot_nt(ops[t][0][:, hs], ops[t][1][:, hs]) for hs in hss] for t in range(nt)]
    sm = [[softmax(ss[t][h], h, ops[t][3], ops[t][4]) for h in range(N_HEADS)] for t in range(nt)]
    for t in range(nt):
        outs = [_dot(sm[t][h][0], ops[t][2][:, hss[h]]) / sm[t][h][1] for h in range(N_HEADS)]
        o_ref[0, t * TILE:(t + 1) * TILE, :] = jnp.concatenate(outs, axis=-1).astype(BF16)
        lse_ref[0, t * TILE:(t + 1) * TILE, :] = jnp.concatenate([sm[t][h][2] for h in range(N_HEADS)], axis=-1)


def _dsw_attention(cv, dil):
    b, l, _ = cv.shape
    lq = min(512, l)
    ncb = A_WIDTH // BRANCH_WIDTH
    qspec = pl.BlockSpec((1, lq, BRANCH_WIDTH), lambda bi, r, i: (bi, i, r * ncb))
    kspec = pl.BlockSpec((1, l, BRANCH_WIDTH), lambda bi, r, i: (bi, 0, r * ncb + 1))
    vspec = pl.BlockSpec((1, l, BRANCH_WIDTH), lambda bi, r, i: (bi, 0, r * ncb + 2))
    ospec = pl.BlockSpec((1, lq, BRANCH_WIDTH), lambda bi, r, i: (bi, i, r))
    o, lse = pl.pallas_call(
        functools.partial(_dsw_kernel, dil=dil, lq=lq),
        grid=(b, dil, l // lq),
        in_specs=[qspec, kspec, vspec],
        out_specs=[ospec, ospec],
        out_shape=[jax.ShapeDtypeStruct((b, l, dil * BRANCH_WIDTH), BF16),
                   jax.ShapeDtypeStruct((b, l, dil * BRANCH_WIDTH), F32)],
        compiler_params=_cparams(("arbitrary", "arbitrary", "arbitrary")),
        name=f"dsw{dil}",
    )(cv, cv, cv)
    return o.reshape(b * l, dil * BRANCH_WIDTH), lse.reshape(b * l, dil * BRANCH_WIDTH)


def _mla_prep_kernel(cq_ref, ckv_ref, kpe_ref, kper_ref, cos_ref, sin_ref, gq_ref, wl_ref, wr_ref,
                     gkv_ref, wk_ref, wv_ref, q_ref, k_ref, v_ref):
    cos = cos_ref[...]
    sin = sin_ref[...]
    cos4 = jnp.concatenate([cos] * N_HEADS, axis=-1)
    sin4 = jnp.concatenate([sin] * N_HEADS, axis=-1)
    cq = cq_ref[...].astype(F32)
    ms = jnp.sum(cq * cq, axis=-1, keepdims=True) * (1.0 / MLA_Q_RANK)
    qn = (cq * lax.rsqrt(ms + RMS_EPS) * gq_ref[...]).astype(BF16)
    scale = (MLA_NOPE_DIM + MLA_ROPE_DIM) ** -0.5 * LOG2E
    q = (_dot(qn, wl_ref[...]) * cos4 + _dot(qn, wr_ref[...]) * sin4) * scale
    q_ref[...] = q.astype(BF16)
    ckv = ckv_ref[...].astype(F32)
    ms = jnp.mean(ckv * ckv, axis=-1, keepdims=True)
    kn = (ckv * lax.rsqrt(ms + RMS_EPS) * gkv_ref[...]).astype(BF16)
    kpe = kpe_ref[...].astype(F32) * cos + kper_ref[...].astype(F32) * sin
    k = _dot(kn, wk_ref[...]) + jnp.concatenate([kpe] * N_HEADS, axis=-1)
    k_ref[...] = k.astype(BF16)
    lane = lax.broadcasted_iota(jnp.int32, (1, N_HEADS * LANE), 1) & (LANE - 1)
    v_ref[...] = (_dot(kn, wv_ref[...]) + jnp.where(lane == HEAD_DIM, 1.0, 0.0)).astype(BF16)


def _mla_prep(c2, s, cos_t, sin_t, gq, wl, wr, gkv, wk, wv):
    n = c2.shape[0]
    tm = 512
    spb = s // tm
    hq = N_HEADS * LANE

    def cspec(off, width):
        return pl.BlockSpec((tm, width), lambda i: (i, off // width))

    tspec = pl.BlockSpec((tm, LANE), lambda i: (i % spb, 0))
    return pl.pallas_call(
        _mla_prep_kernel,
        grid=(n // tm,),
        in_specs=[cspec(C_BCQ, 256), cspec(C_BCKV, 128), cspec(C_BKPE, 128), cspec(C_BKPEROT, 128),
                  tspec, tspec,
                  _const_spec((1, 256)), _const_spec((256, hq)), _const_spec((256, hq)),
                  _const_spec((1, 128)), _const_spec((128, hq)), _const_spec((128, hq))],
        out_specs=[pl.BlockSpec((tm, hq), lambda i: (i, 0)),
                   pl.BlockSpec((tm, hq), lambda i: (i, 0)),
                   pl.BlockSpec((tm, hq), lambda i: (i, 0))],
        out_shape=[jax.ShapeDtypeStruct((n, hq), BF16),
                   jax.ShapeDtypeStruct((n, hq), BF16),
                   jax.ShapeDtypeStruct((n, hq), BF16)],
        compiler_params=_cparams(("arbitrary",)),
        name="mla_prep",
    )(c2, c2, c2, c2, cos_t, sin_t, gq, wl, wr, gkv, wk, wv)


def _mla_flash_kernel(q_ref, k_ref, v_ref, o_ref):
    i = pl.program_id(1)
    t0 = i * MLA_TQ
    row = lax.broadcasted_iota(jnp.int32, (MLA_TQ, MLA_TK), 0)
    col = lax.broadcasted_iota(jnp.int32, (MLA_TQ, MLA_TK), 1)

    def step(j, carry, masked):
        ks = pl.multiple_of(j * MLA_TK, MLA_TK)
        if masked:
            keep = (col - row) <= (t0 - ks)
        hls = [slice(h * LANE, (h + 1) * LANE) for h in range(N_HEADS)]
        ss = [_dot_nt(q_ref[0, :, hl], k_ref[0, pl.ds(ks, MLA_TK), hl]) for hl in hls]
        ps, ms = [], []
        for h in range(N_HEADS):
            s = jnp.where(keep, ss[h], NEG) if masked else ss[h]
            m_new = jnp.maximum(carry[h][0], jnp.max(s, axis=-1, keepdims=True))
            ps.append(jnp.exp2(s - m_new).astype(BF16))
            ms.append(m_new)
        new = []
        for h in range(N_HEADS):
            m, acc = carry[h]
            acc = jnp.exp2(m - ms[h]) * acc + _dot(ps[h], v_ref[0, pl.ds(ks, MLA_TK), hls[h]])
            new.append((ms[h], acc))
        return tuple(new)

    init = tuple((jnp.full((MLA_TQ, 1), NEG, F32), jnp.zeros((MLA_TQ, LANE), F32)) for _ in range(N_HEADS))
    nfull = t0 // MLA_TK
    carry = lax.fori_loop(0, nfull, functools.partial(step, masked=False), init)
    carry = step(nfull, carry, True)
    o_ref[0] = jnp.concatenate([acc[:, 0:HEAD_DIM] / acc[:, HEAD_DIM:HEAD_DIM + 1] for (_, acc) in carry],
                               axis=-1).astype(BF16)


def _mla_flash(q3, k3, v3):
    b, s, hq = q3.shape
    return pl.pallas_call(
        _mla_flash_kernel,
        grid=(b, s // MLA_TQ),
        in_specs=[pl.BlockSpec((1, MLA_TQ, hq), lambda bi, i: (bi, i, 0)),
                  pl.BlockSpec((1, s, hq), lambda bi, i: (bi, 0, 0)),
                  pl.BlockSpec((1, s, hq), lambda bi, i: (bi, 0, 0))],
        out_specs=pl.BlockSpec((1, MLA_TQ, BRANCH_WIDTH), lambda bi, i: (bi, i, 0)),
        out_shape=jax.ShapeDtypeStruct((b, s, BRANCH_WIDTH), BF16),
        compiler_params=_cparams(("arbitrary", "arbitrary")),
        name="mla_flash",
    )(q3, k3, v3)


def _nsa_cmp_kernel(ch_ref, pt_ref, pb_ref, wt_ref, wb_ref, b1_ref, w2_ref, b2_ref, o_ref):
    ch = ch_ref[0]
    top = _dot(ch, wt_ref[...])
    bot = _dot(ch, wb_ref[...])
    const = (_dot(pt_ref[...], wt_ref[...]) + _dot(pb_ref[...], wb_ref[...]))[0:1, :] + b1_ref[...]
    n = ch.shape[0]
    hid = top + pltpu.roll(bot, n - 1, 0) + const
    hid = jax.nn.gelu(hid)
    o_ref[0] = (_dot(hid.astype(BF16), w2_ref[...]) + b2_ref[...]).astype(BF16)


def _nsa_compress(kvc3, pos_t, pos_b, wt, wb, b1, w2, b2):
    b, s, _ = kvc3.shape
    nch = s // NSA_CMP_STRIDE
    cw = NSA_CMP_STRIDE * 128
    ch = kvc3.reshape(b, nch, cw)
    return pl.pallas_call(
        _nsa_cmp_kernel,
        grid=(b,),
        in_specs=[pl.BlockSpec((1, nch, cw), lambda bi: (bi, 0, 0)),
                  _const_spec((8, cw)), _const_spec((8, cw)),
                  _const_spec((cw, 2 * NSA_CMP_HIDDEN)), _const_spec((cw, 2 * NSA_CMP_HIDDEN)),
                  _const_spec((1, 2 * NSA_CMP_HIDDEN)),
                  _const_spec((2 * NSA_CMP_HIDDEN, 2 * NSA_KV_DIM)), _const_spec((1, 2 * NSA_KV_DIM))],
        out_specs=pl.BlockSpec((1, nch, 2 * NSA_KV_DIM), lambda bi: (bi, 0, 0)),
        out_shape=jax.ShapeDtypeStruct((b, nch, 2 * NSA_KV_DIM), BF16),
        compiler_params=_cparams(("arbitrary",)),
        name="nsa_compress",
    )(ch, pos_t, pos_b, wt, wb, b1, w2, b2)


def _split_dot_nt(w, x, terms=3):
    out = None
    rem = x
    for _ in range(terms):
        part = rem.astype(BF16)
        rem = rem - part.astype(F32)
        y = _dot_nt(w, part)
        out = y if out is None else out + y
    return out


def _nsa_sel_table(s):
    t = np.arange(s)
    tab = np.zeros((s, 2 * LANE), np.float32)
    tab[:, HEAD_DIM] = t >> 7
    tab[:, HEAD_DIM + 1] = t & 127
    tab[t, LANE + t // NSA_SEL_BLOCK] = NEG
    return jnp.asarray(tab, dtype=BF16)


def _nsa_cmp_table(ncmp):
    end = np.arange(ncmp) * NSA_CMP_STRIDE + NSA_CMP_BLOCK - 1
    tab = np.zeros((ncmp, LANE), np.float32)
    tab[:, HEAD_DIM] = end >> 7
    tab[:, HEAD_DIM + 1] = end & 127
    return jnp.asarray(tab, dtype=BF16)


def _nsa_kernel(q_ref, g_ref, cmp_ref, sel_ref, win_ref, tab_ref, ctab_ref, o_ref, *, ncmp):
    i = pl.program_id(1)
    t0 = i * TILE
    hr = N_HEADS * TILE
    q = q_ref[0]
    q4 = jnp.concatenate([q[:, h * HEAD_DIM:(h + 1) * HEAD_DIM] for h in range(N_HEADS)], axis=0)
    rowh = lax.broadcasted_iota(jnp.int32, (hr, 1), 0)
    hidx = rowh >> 7
    r = rowh & (TILE - 1)
    slope = jnp.where(hidx == 0, SLOPES_C[0],
                      jnp.where(hidx == 1, SLOPES_C[1], jnp.where(hidx == 2, SLOPES_C[2], SLOPES_C[3]))).astype(F32)

    lane64 = lax.broadcasted_iota(jnp.int32, (hr, HEAD_DIM), 1)
    alibi_q = jnp.where(lane64 == 0, slope * 128.0, jnp.where(lane64 == 1, slope, 0.0)).astype(BF16)
    q4a = jnp.concatenate([q4, alibi_q], axis=1)
    lane = lax.broadcasted_iota(jnp.int32, (1, LANE), 1)
    keep_k = jnp.where(lane < NSA_KV_DIM, 1.0, 0.0).astype(BF16)
    keep_v = jnp.where(lane >= NSA_KV_DIM, 1.0, 0.0).astype(BF16)
    one0 = jnp.where(lane == 0, 1.0, 0.0).astype(BF16)

    def value_operand(kv):
        return kv * keep_v + one0

    def masked_softmax(s, mask):
        sb = jnp.where(mask, s, NEG)
        m = jnp.maximum(jnp.max(sb, axis=-1, keepdims=True), 0.1 * NEG)
        return jnp.exp(sb - m)

    def normalise(pv):
        return pv[:, NSA_KV_DIM:2 * NSA_KV_DIM], jnp.maximum(pv[:, 0:1], 1e-30)

    ckv = cmp_ref[0]
    ncol = lax.broadcasted_iota(jnp.int32, (1, ncmp), 1)
    last_ok = (t0 + r - (NSA_CMP_BLOCK - 1)) >> 4
    s_cmp = _dot_nt(q4a, ckv * keep_k + ctab_ref[...])
    wlen = NSA_WINDOW + TILE
    kstart = pl.multiple_of(jnp.maximum(t0 - NSA_WINDOW, 0), TILE)
    wkv = win_ref[0, pl.ds(kstart, wlen), :]
    s_win = _dot_nt(q4a, wkv * keep_k + tab_ref[pl.ds(kstart, wlen), 0:LANE])
    e_cmp = masked_softmax(s_cmp, ncol <= last_ok)
    den_c = jnp.maximum(jnp.sum(e_cmp, axis=-1, keepdims=True), 1e-30)
    p_cmp = e_cmp / den_c
    o_cmp = _dot(p_cmp.astype(BF16), ckv)[:, NSA_KV_DIM:2 * NSA_KV_DIM]
    psum = p_cmp[0:TILE] + p_cmp[TILE:2 * TILE] + p_cmp[2 * TILE:3 * TILE] + p_cmp[3 * TILE:4 * TILE]
    jrow = lax.broadcasted_iota(jnp.int32, (TILE, ncmp), 0)
    ncol2 = lax.broadcasted_iota(jnp.int32, (TILE, ncmp), 1)
    delta = jrow * (NSA_SEL_BLOCK // NSA_CMP_STRIDE) - ncol2
    selmap = jnp.where((delta == 0) | (delta == 4), 1.0,
                       jnp.where((delta > 0) & (delta < 4), 2.0, 0.0)).astype(BF16)
    p_sel = _split_dot_nt(selmap, psum)

    blk = lax.broadcasted_iota(jnp.int32, (TILE, TILE), 0)
    tq = t0 + lax.broadcasted_iota(jnp.int32, (TILE, TILE), 1)
    cur = tq >> 6
    forced = (blk == 0) | (blk == cur) | (blk == cur - 1)
    cand = jnp.where((blk <= cur) & jnp.logical_not(forced), p_sel, -1.0)
    picked = jnp.where(forced, 1.0, 0.0)
    for _ in range(NSA_TOP_N - 3):
        m = jnp.max(cand, axis=0, keepdims=True)
        idx = jnp.min(jnp.where(cand == m, blk, TILE), axis=0, keepdims=True)
        hit = (blk == idx) & (m >= 0.0)
        picked = jnp.where(hit, 1.0, picked)
        cand = jnp.where(hit, -1.0, cand)
    not_sel = jnp.transpose(1.0 - picked).astype(BF16)

    q4aug = jnp.concatenate([q4a, jnp.concatenate([not_sel] * N_HEADS, axis=0)], axis=1)
    colk = lax.broadcasted_iota(jnp.int32, (hr, NSA_TK), 1)

    def sel_step(j, carry, diag):
        m, acc = carry
        ks = pl.multiple_of(j * NSA_TK, NSA_TK)
        kv = sel_ref[0, pl.ds(ks, NSA_TK), :]
        tb = tab_ref[pl.ds(ks, NSA_TK), :]
        kaug = jnp.concatenate([kv * keep_k + tb[:, 0:LANE], tb[:, LANE:2 * LANE]], axis=1)
        s = _dot_nt(q4aug, kaug)
        if diag:
            s = jnp.where((colk - r) <= (t0 - ks), s, NEG)
        m_new = jnp.maximum(m, jnp.max(s, axis=-1, keepdims=True))
        p = jnp.exp(s - m_new)
        acc = jnp.exp(m - m_new) * acc + _dot(p.astype(BF16), value_operand(kv))
        return m_new, acc

    init = (jnp.full((hr, 1), NEG, F32), jnp.zeros((hr, LANE), F32))
    jd = t0 // NSA_TK
    carry = sel_step(jd, init, True)
    _, acc = lax.fori_loop(0, jd, functools.partial(sel_step, diag=False), carry)
    o_slc, den_s = normalise(acc)
    o_slc = o_slc / den_s

    colw = lax.broadcasted_iota(jnp.int32, (hr, wlen), 1)
    dist_w = (t0 - kstart) + r - colw
    e_win = masked_softmax(s_win, (dist_w >= 0) & (dist_w < NSA_WINDOW))
    o_win, den_w = normalise(_dot(e_win.astype(BF16), value_operand(wkv)))
    o_win = o_win / den_w

    g = jax.nn.sigmoid(g_ref[0].astype(F32))

    def gate_col(kk):
        return jnp.concatenate([g[:, 3 * h + kk:3 * h + kk + 1] for h in range(N_HEADS)], axis=0)

    out4 = gate_col(0) * o_cmp + gate_col(1) * o_slc + gate_col(2) * o_win
    o_ref[0] = jnp.concatenate([out4[h * TILE:(h + 1) * TILE] for h in range(N_HEADS)], axis=-1).astype(BF16)


def _nsa_attention(c3, cmp3):
    b, s, _ = c3.shape
    ncmp = cmp3.shape[1]
    return pl.pallas_call(
        functools.partial(_nsa_kernel, ncmp=ncmp),
        grid=(b, s // TILE),
        in_specs=[pl.BlockSpec((1, TILE, BRANCH_WIDTH), lambda bi, i: (bi, i, C_CQ // BRANCH_WIDTH)),
                  pl.BlockSpec((1, TILE, 128), lambda bi, i: (bi, i, C_CG // 128)),
                  pl.BlockSpec((1, ncmp, 128), lambda bi, i: (bi, 0, 0)),
                  pl.BlockSpec((1, s, 128), lambda bi, i: (bi, 0, C_CSEL // 128)),
                  pl.BlockSpec((1, s, 128), lambda bi, i: (bi, 0, C_CWIN // 128)),
                  _const_spec((s, 2 * LANE)), _const_spec((ncmp, LANE))],
        out_specs=pl.BlockSpec((1, TILE, BRANCH_WIDTH), lambda bi, i: (bi, i, 0)),
        out_shape=jax.ShapeDtypeStruct((b, s, BRANCH_WIDTH), BF16),
        compiler_params=_cparams(("arbitrary", "arbitrary")),
        name="nsa_attn",
    )(c3, c3, cmp3, c3, c3, _nsa_sel_table(s), _nsa_cmp_table(ncmp))


def _sb_kernel(q_ref, k_ref, v_ref, o_ref, run_ref, acc_ref):
    i = pl.program_id(1)
    t0 = i * SB_TQ
    row = lax.broadcasted_iota(jnp.int32, (SB_TQ, TILE), 0)
    col = lax.broadcasted_iota(jnp.int32, (SB_TQ, TILE), 1)
    urow = lax.broadcasted_iota(jnp.int32, (2 * TILE, 2 * TILE), 0) & (TILE - 1)
    ucol = lax.broadcasted_iota(jnp.int32, (2 * TILE, 2 * TILE), 1)
    umat = jnp.where((ucol >= TILE) | (urow > ucol), 1.0, 0.0).astype(BF16)
    run_ref[...] = jnp.zeros_like(run_ref)
    acc_ref[...] = jnp.zeros_like(acc_ref)

    def tile(j, masked):
        ks = pl.multiple_of(j * TILE, TILE)
        if masked:
            strict = (col - row) < (t0 - ks)
        kt = k_ref[0, pl.ds(ks, TILE), :]
        vt = v_ref[0, pl.ds(ks, TILE), :]
        hss = [slice(h * HEAD_DIM, (h + 1) * HEAD_DIM) for h in range(N_HEADS)]
        zs = [_dot_nt(q_ref[0, :, hs], kt[:, hs]) for hs in hss]
        lss, css = [], []
        for z in zs:
            ls = jnp.minimum(z, 0.0) - jnp.log(1.0 + jnp.exp(-jnp.abs(z)))
            l1m = ls - z
            if masked:
                l1m = jnp.where(strict, l1m, 0.0)
            hi = l1m.astype(BF16)
            lo = (l1m - hi.astype(F32)).astype(BF16)
            lss.append(ls)
            css.append(_dot(jnp.concatenate([hi, lo], axis=-1), umat))
        for h in range(N_HEADS):
            run = run_ref[h]
            a = jnp.exp(lss[h] + run + css[h][:, 0:TILE])
            if masked:
                a = jnp.where(strict, a, 0.0)
            acc_ref[h] += _dot(a.astype(BF16), vt[:, hss[h]])
            run_ref[h] = run + css[h][:, TILE:2 * TILE]

    def run_max():
        return jnp.max(jnp.maximum(jnp.maximum(run_ref[0], run_ref[1]), jnp.maximum(run_ref[2], run_ref[3])))

    j_top = t0 // TILE + SB_TQ // TILE - 1
    for dj in range(SB_TQ // TILE):
        tile(j_top - dj, True)

    def cond(c):
        return (c[0] >= 0) & (c[1] > SB_UNDERFLOW)

    def body(c):
        tile(c[0], False)
        return c[0] - 1, run_max()

    lax.while_loop(cond, body, (j_top - SB_TQ // TILE, run_max()))
    o_ref[0] = jnp.concatenate([acc_ref[h] for h in range(N_HEADS)], axis=-1).astype(BF16)


def _sb_attention(c3):
    b, s, _ = c3.shape
    return pl.pallas_call(
        _sb_kernel,
        grid=(b, s // SB_TQ),
        in_specs=[pl.BlockSpec((1, SB_TQ, BRANCH_WIDTH), lambda bi, i: (bi, i, C_DQ // BRANCH_WIDTH)),
                  pl.BlockSpec((1, s, BRANCH_WIDTH), lambda bi, i: (bi, 0, C_DK // BRANCH_WIDTH)),
                  pl.BlockSpec((1, s, BRANCH_WIDTH), lambda bi, i: (bi, 0, C_DV // BRANCH_WIDTH))],
        out_specs=pl.BlockSpec((1, SB_TQ, BRANCH_WIDTH), lambda bi, i: (bi, i, 0)),
        out_shape=jax.ShapeDtypeStruct((b, s, BRANCH_WIDTH), BF16),
        scratch_shapes=[pltpu.VMEM((N_HEADS, SB_TQ, TILE), F32), pltpu.VMEM((N_HEADS, SB_TQ, HEAD_DIM), F32)],
        compiler_params=_cparams(("arbitrary", "arbitrary")),
        name="stickbreak",
    )(c3, c3, c3)


def _merge_kernel(x_ref, ng_ref, o1_ref, o2_ref, o3_ref, l1_ref, l2_ref, l3_ref, ga_ref,
                  yb_ref, gb_ref, yc_ref, gc_ref, yd_ref, gd_ref,
                  wm_ref, bm_ref, wb_ref, wo_ref, fg_ref, out_ref, *scratch, final):
    x = x_ref[...]
    ms = jnp.mean(x * x, axis=-1, keepdims=True)
    h = (x * lax.rsqrt(ms + RMS_EPS) * ng_ref[...]).astype(BF16)

    def token_order(ref, scr, dil):
        rows = x.shape[0] // dil
        for r in range(dil):
            for jl in range(BRANCH_WIDTH // LANE):
                lo = r * BRANCH_WIDTH + jl * LANE
                scr[jl, pl.ds(r, rows, stride=dil), :] = ref[:, lo:lo + LANE].astype(F32)
        return jnp.concatenate([scr[jl] for jl in range(BRANCH_WIDTH // LANE)], axis=-1)

    o2, l2 = token_order(o2_ref, scratch[0], 4), token_order(l2_ref, scratch[1], 4)
    o3, l3 = token_order(o3_ref, scratch[2], 16), token_order(l3_ref, scratch[3], 16)
    l1 = l1_ref[...]
    lm = jnp.maximum(jnp.maximum(l1, l2), l3)
    e1, e2, e3 = jnp.exp(l1 - lm), jnp.exp(l2 - lm), jnp.exp(l3 - lm)
    ya = (e1 * o1_ref[...].astype(F32) + e2 * o2 + e3 * o3) / (e1 + e2 + e3)
    ys = (ya, yb_ref[...].astype(F32), yc_ref[...].astype(F32), yd_ref[...].astype(F32))
    gs = (ga_ref, gb_ref, gc_ref, gd_ref)
    merged = None
    for i in range(4):
        y = (ys[i] * jax.nn.silu(gs[i][...].astype(F32))).astype(BF16)
        gate = jax.nn.sigmoid(_dot(h, wm_ref[i]) + bm_ref[i])
        term = gate * _dot(y, wb_ref[i])
        merged = term if merged is None else merged + term
    xn = x + _dot(merged.astype(BF16), wo_ref[...])
    if final:
        ms = jnp.mean(xn * xn, axis=-1, keepdims=True)
        xn = xn * lax.rsqrt(ms + RMS_EPS) * fg_ref[...]
    out_ref[...] = xn


def _merge(x2, ng, a_outs, a_lses, c2, yb, yc, yd, wm, bm, wb, wo, fg, final):
    n, d = x2.shape
    tm = 256
    bw = BRANCH_WIDTH

    def rows(width):
        return pl.BlockSpec((tm, width), lambda i: (i, 0))

    def cspec(off):
        return pl.BlockSpec((tm, bw), lambda i: (i, off // bw))

    def dspec(dil):
        return pl.BlockSpec((tm // dil, dil * bw), lambda i: (i, 0))

    return pl.pallas_call(
        functools.partial(_merge_kernel, final=final),
        grid=(n // tm,),
        in_specs=[rows(d), _const_spec((1, d)),
                  dspec(1), dspec(4), dspec(16), dspec(1), dspec(4), dspec(16), cspec(C_AGATE),
                  rows(bw), cspec(C_BGATE), rows(bw), cspec(C_CGATE), rows(bw), cspec(C_DGATE),
                  _const_spec((4, d, d)), _const_spec((4, 1, d)), _const_spec((4, bw, d)),
                  _const_spec((d, d)), _const_spec((1, d))],
        out_specs=rows(d),
        out_shape=jax.ShapeDtypeStruct((n, d), F32),
        scratch_shapes=[pltpu.VMEM((bw // LANE, tm, LANE), F32) for _ in range(4)],
        compiler_params=_cparams(("arbitrary",)),
        name="merge",
    )(x2, ng, *a_outs, *a_lses, c2, yb, c2, yc, c2, yd, c2, wm, bm, wb, wo, fg)


def _split_w_in(w):
    widths = (256, 256, 256, 256, MLA_Q_RANK, MLA_KV_RANK, MLA_ROPE_DIM, 256,
              256, 64, 64, 64, 64, 64, 64, 3 * N_HEADS, 256, 256, 256, 256, 256)
    names = ("a_q", "a_k", "a_v", "a_gate", "b_cq", "b_ckv", "b_kpe", "b_gate",
             "c_q", "c_kc", "c_vc", "c_ks", "c_vs", "c_kw", "c_vw", "c_g", "c_gate",
             "d_q", "d_k", "d_v", "d_gate")
    out, off = {}, 0
    for nm, wd in zip(names, widths):
        out[nm] = w[:, off:off + wd]
        off += wd
    return out


def _rot_cols(w):
    half = w.shape[1] // 2
    return jnp.concatenate([-w[:, half:], w[:, :half]], axis=1)


def _prep_w_in(w):
    p = _split_w_in(w)
    d = w.shape[0]
    qs = HEAD_DIM ** -0.5

    def z(n):
        return jnp.zeros((d, n), w.dtype)

    kpe_blk = jnp.concatenate([z(ROPE_LANE), p["b_kpe"], z(LANE - ROPE_LANE - MLA_ROPE_DIM)], axis=1)
    kper_blk = jnp.concatenate([z(ROPE_LANE), _rot_cols(p["b_kpe"]), z(LANE - ROPE_LANE - MLA_ROPE_DIM)], axis=1)
    cols = [p["a_q"] * qs, p["a_k"], p["a_v"],
            p["a_gate"], p["b_cq"], z(256 - MLA_Q_RANK), p["b_ckv"], kpe_blk, kper_blk,
            p["c_g"], z(128 - 3 * N_HEADS), p["b_gate"],
            p["c_q"] * qs, p["c_ks"], p["c_vs"], p["c_kw"], p["c_vw"], p["c_gate"],
            p["d_q"] * qs, p["d_k"], p["d_v"], p["d_gate"],
            p["c_kc"], p["c_vc"]]
    out = jnp.concatenate(cols, axis=1)
    assert out.shape[1] == W_WIDTH
    return out.astype(BF16)


def _prep_mla(w_uq, w_ukv, gq):
    qd = MLA_NOPE_DIM + MLA_ROPE_DIM
    pad = LANE - qd
    wl, wr, wk, wv = [], [], [], []
    for h in range(N_HEADS):
        nope = w_uq[:, h * qd:h * qd + MLA_NOPE_DIM]
        rp = w_uq[:, h * qd + MLA_NOPE_DIM:(h + 1) * qd]
        zq = jnp.zeros((MLA_Q_RANK, pad), w_uq.dtype)
        wl += [nope, rp, zq]
        wr += [jnp.zeros_like(nope), _rot_cols(rp), zq]
        kn = w_ukv[:, h * 128:h * 128 + MLA_NOPE_DIM]
        wk += [kn, jnp.zeros((MLA_KV_RANK, LANE - MLA_NOPE_DIM), w_ukv.dtype)]
        wv += [w_ukv[:, h * 128 + MLA_NOPE_DIM:(h + 1) * 128],
               jnp.zeros((MLA_KV_RANK, LANE - HEAD_DIM), w_ukv.dtype)]
    rpad = ((0, 256 - MLA_Q_RANK), (0, 0))
    wl = jnp.pad(jnp.concatenate(wl, axis=1), rpad).astype(BF16)
    wr = jnp.pad(jnp.concatenate(wr, axis=1), rpad).astype(BF16)
    gq = jnp.pad(gq, (0, 256 - MLA_Q_RANK)).reshape(1, 256)
    return wl, wr, jnp.concatenate(wk, axis=1).astype(BF16), jnp.concatenate(wv, axis=1).astype(BF16), gq


def _rope_tables(s):
    half = MLA_ROPE_DIM // 2
    inv = ROPE_THETA ** (-jnp.arange(half, dtype=F32) / half)
    ang = jnp.arange(s, dtype=F32)[:, None] * inv[None, :]
    cos, sin = jnp.cos(ang), jnp.sin(ang)
    ones = jnp.ones((s, ROPE_LANE), F32)
    zl = jnp.zeros((s, ROPE_LANE), F32)
    zr = jnp.zeros((s, LANE - ROPE_LANE - MLA_ROPE_DIM), F32)
    return (jnp.concatenate([ones, cos, cos, zr], axis=1), jnp.concatenate([zl, sin, sin, zr], axis=1))


def _prep_nsa(pos, w1, b1, w2, b2):
    half = NSA_CMP_BLOCK // 2
    eye = jnp.eye(2, dtype=w1.dtype)
    w1r = w1.reshape(2, NSA_CMP_BLOCK, NSA_KV_DIM, NSA_CMP_HIDDEN)
    posr = pos

    def comb_w(part):
        return jnp.einsum("ktdh,kj->tkdjh", part, eye).reshape(half * 2 * NSA_KV_DIM, 2 * NSA_CMP_HIDDEN)

    def comb_p(part):
        flat = jnp.transpose(part, (1, 0, 2)).reshape(1, half * 2 * NSA_KV_DIM)
        return jnp.pad(flat, ((0, 7), (0, 0))).astype(BF16)

    wt, wb = comb_w(w1r[:, :half]).astype(BF16), comb_w(w1r[:, half:]).astype(BF16)
    pt, pb = comb_p(posr[:, :half]), comb_p(posr[:, half:])
    b1c = b1.reshape(1, 2 * NSA_CMP_HIDDEN)
    zero = jnp.zeros((NSA_CMP_HIDDEN, NSA_KV_DIM), w2.dtype)
    w2c = jnp.concatenate([jnp.concatenate([w2[0], zero], axis=1),
                           jnp.concatenate([zero, w2[1]], axis=1)], axis=0).astype(BF16)
    b2c = b2.reshape(1, 2 * NSA_KV_DIM)
    return pt, pb, wt, wb, b1c, w2c, b2c


def kernel(x, norm_g, w_in, mla_q_norm, mla_w_uq, mla_kv_norm, mla_w_ukv, nsa_pos, nsa_w1, nsa_b1,
           nsa_w2, nsa_b2, w_branch, w_merge, b_merge, w_out, final_norm_g):
    b, s, d = x.shape
    depth = norm_g.shape[0]
    n = b * s
    assert s % (TILE * max(DSW_DILATIONS)) == 0 and s >= NSA_WINDOW + TILE
    cos_t, sin_t = _rope_tables(s)
    x2 = x.reshape(n, d)
    for layer in range(depth):
        *a_cls, c2, kvc = _inproj(x2, norm_g[layer].reshape(1, d), _prep_w_in(w_in[layer]))
        c3 = c2.reshape(b, s, C_WIDTH)
        a_res = [_dsw_attention(a.reshape(b, s // dil, dil * A_WIDTH), dil) for a, dil in zip(a_cls, DSW_DILATIONS)]
        wl, wr, wk, wv, gq = _prep_mla(mla_w_uq[layer], mla_w_ukv[layer], mla_q_norm[layer])
        qb, kb, vb = _mla_prep(c2, s, cos_t, sin_t, gq, wl, wr, mla_kv_norm[layer].reshape(1, MLA_KV_RANK), wk, wv)
        hq = N_HEADS * LANE
        yb = _mla_flash(qb.reshape(b, s, hq), kb.reshape(b, s, hq), vb.reshape(b, s, hq))
        cmp3 = _nsa_compress(kvc.reshape(b, s, 128),
                             *_prep_nsa(nsa_pos[layer], nsa_w1[layer], nsa_b1[layer], nsa_w2[layer], nsa_b2[layer]))
        yc = _nsa_attention(c3, cmp3)
        yd = _sb_attention(c3)
        x2 = _merge(x2, norm_g[layer].reshape(1, d), [r[0] for r in a_res], [r[1] for r in a_res], c2,
                    yb.reshape(n, BRANCH_WIDTH), yc.reshape(n, BRANCH_WIDTH), yd.reshape(n, BRANCH_WIDTH),
                    w_merge[layer].astype(BF16), b_merge[layer].reshape(4, 1, d),
                    w_branch[layer].astype(BF16), w_out[layer].astype(BF16),
                    final_norm_g.reshape(1, d), final=(layer == depth - 1))
    return x2.reshape(b, s, d)
```

```python
import functools

import numpy as np
import jax
import jax.numpy as jnp
from jax import lax
from jax.experimental import pallas as pl
from jax.experimental.pallas import tpu as pltpu

F32 = jnp.float32
BF16 = jnp.bfloat16

HEAD_DIM = 64
N_HEADS = 4
BRANCH_WIDTH = N_HEADS * HEAD_DIM
RMS_EPS = 1e-6
DSW_DILATIONS = (1, 4, 16)
DSW_SPAN = 128
MLA_Q_RANK = 192
MLA_KV_RANK = 128
MLA_NOPE_DIM = 64
MLA_ROPE_DIM = 32
ROPE_THETA = 10000.0
NSA_CMP_BLOCK = 32
NSA_CMP_STRIDE = 16
NSA_CMP_HIDDEN = 128
NSA_SEL_BLOCK = 64
NSA_TOP_N = 16
NSA_WINDOW = 512
NSA_KV_DIM = 64
_ALIBI = [2.0 ** (-(i + 1)) for i in range(8)]
SLOPES_A = _ALIBI[0::2]
SLOPES_C = _ALIBI[1::2]

LANE = 128
TILE = 128
VMEM_LIMIT = 56 * 1024 * 1024
NEG = -1e30
LOG2E = 1.4426950408889634
MLA_TQ, MLA_TK = 256, 1024
SB_TQ = 256
SB_UNDERFLOW = -104.0

A_WIDTH = 3 * 256
C_AGATE = 0
C_BCQ, C_BCKV, C_BKPE, C_BKPEROT = 256, 512, 640, 768
C_CG, C_BGATE, C_CQ, C_CSEL, C_CWIN, C_CGATE = 896, 1024, 1280, 1536, 1664, 1792
C_DQ, C_DK, C_DV, C_DGATE = 2048, 2304, 2560, 2816
C_WIDTH = 3072
W_C = A_WIDTH
W_KVC = A_WIDTH + C_WIDTH
W_WIDTH = W_KVC + 128
NSA_TK = 1024
NSA_TQ = 256
ROPE_LANE = 64


def _dot(a, b):
    return jnp.dot(a, b, preferred_element_type=F32)


def _dot_nt(a, b):
    return lax.dot_general(a, b, (((1,), (1,)), ((), ())), preferred_element_type=F32)


def _cparams(sem):
    return pltpu.CompilerParams(dimension_semantics=sem, vmem_limit_bytes=VMEM_LIMIT)


def _const_spec(shape):
    nd = len(shape)
    return pl.BlockSpec(shape, lambda *_: (0,) * nd)


def _inproj_kernel(x_ref, g_ref, w_ref, a1_ref, a4_ref, a16_ref, c_ref, kvc_ref, a_scr):
    x = x_ref[...]
    ms = jnp.mean(x * x, axis=-1, keepdims=True)
    h = (x * lax.rsqrt(ms + RMS_EPS) * g_ref[...]).astype(BF16)
    cw = 256
    for j in range(A_WIDTH // cw):
        y = _dot(h, w_ref[:, j * cw:(j + 1) * cw])
        a1_ref[:, j * cw:(j + 1) * cw] = y.astype(BF16)
        for jj in range(cw // LANE):
            a_scr[(j * cw) // LANE + jj] = y[:, jj * LANE:(jj + 1) * LANE]
    tm = a_scr.shape[1]
    for dil, ref in ((4, a4_ref), (16, a16_ref)):
        for r in range(dil):
            for jl in range(A_WIDTH // LANE):
                rows = a_scr[jl, pl.ds(r, tm // dil, stride=dil), :]
                ref[:, r * A_WIDTH + jl * LANE:r * A_WIDTH + (jl + 1) * LANE] = rows.astype(BF16)
    for j in range(C_WIDTH // cw):
        c_ref[:, j * cw:(j + 1) * cw] = _dot(h, w_ref[:, W_C + j * cw:W_C + (j + 1) * cw]).astype(BF16)
    kvc_ref[...] = _dot(h, w_ref[:, W_KVC:W_KVC + 128]).astype(BF16)


def _inproj(x2, g, w):
    n, d = x2.shape
    tm = 512
    return pl.pallas_call(
        _inproj_kernel,
        grid=(n // tm,),
        in_specs=[pl.BlockSpec((tm, d), lambda i: (i, 0)),
                  _const_spec((1, d)),
                  _const_spec((d, W_WIDTH))],
        out_specs=[pl.BlockSpec((tm // dil, dil * A_WIDTH), lambda i: (i, 0)) for dil in DSW_DILATIONS]
        + [pl.BlockSpec((tm, C_WIDTH), lambda i: (i, 0)),
           pl.BlockSpec((tm, 128), lambda i: (i, 0))],
        out_shape=[jax.ShapeDtypeStruct((n // dil, dil * A_WIDTH), BF16) for dil in DSW_DILATIONS]
        + [jax.ShapeDtypeStruct((n, C_WIDTH), BF16),
           jax.ShapeDtypeStruct((n, 128), BF16)],
        scratch_shapes=[pltpu.VMEM((A_WIDTH // LANE, tm, LANE), F32)],
        compiler_params=_cparams(("arbitrary",)),
        name="inproj",
    )(x2, g, w)


def _dsw_kernel(q_ref, k_ref, v_ref, o_ref, lse_ref, *, dil, lq):
    blk = pl.program_id(2)
    row = lax.broadcasted_iota(jnp.int32, (TILE, 2 * TILE), 0)
    col = lax.broadcasted_iota(jnp.int32, (TILE, 2 * TILE), 1)
    hss = [slice(h * HEAD_DIM, (h + 1) * HEAD_DIM) for h in range(N_HEADS)]
    nt = lq // TILE

    def operands(t):
        u0 = blk * lq + t * TILE
        kstart = pl.multiple_of(jnp.maximum(u0 - TILE, 0), TILE)
        d = (u0 - kstart) + row - col
        valid = (d >= 0) & (d <= DSW_SPAN)
        return (q_ref[0, t * TILE:(t + 1) * TILE, :], k_ref[0, pl.ds(kstart, 2 * TILE), :],
                v_ref[0, pl.ds(kstart, 2 * TILE), :], valid, d.astype(F32))

    def softmax(s, h, valid, df):
        s = jnp.where(valid, s - (SLOPES_A[h] * dil) * df, NEG)
        m = jnp.max(s, axis=-1, keepdims=True)
        p = jnp.exp(s - m)
        l = jnp.sum(p, axis=-1, keepdims=True)
        return p.astype(BF16), l, jnp.broadcast_to(m + jnp.log(l), (TILE, HEAD_DIM))

    ops = [operands(t) for t in range(nt)]
    ss = [[_dot_nt(ops[t][0][:, hs], ops[t][1][:, hs]) for hs in hss] for t in range(nt)]
    sm = [[softmax(ss[t][h], h, ops[t][3], ops[t][4]) for h in range(N_HEADS)] for t in range(nt)]
    for t in range(nt):
        outs = [_dot(sm[t][h][0], ops[t][2][:, hss[h]]) / sm[t][h][1] for h in range(N_HEADS)]
        o_ref[0, t * TILE:(t + 1) * TILE, :] = jnp.concatenate(outs, axis=-1).astype(BF16)
        lse_ref[0, t * TILE:(t + 1) * TILE, :] = jnp.concatenate([sm[t][h][2] for h in range(N_HEADS)], axis=-1)


def _dsw_attention(cv, dil):
    b, l, _ = cv.shape
    lq = min(512, l)
    ncb = A_WIDTH // BRANCH_WIDTH
    qspec = pl.BlockSpec((1, lq, BRANCH_WIDTH), lambda bi, r, i: (bi, i, r * ncb))
    kspec = pl.BlockSpec((1, l, BRANCH_WIDTH), lambda bi, r, i: (bi, 0, r * ncb + 1))
    vspec = pl.BlockSpec((1, l, BRANCH_WIDTH), lambda bi, r, i: (bi, 0, r * ncb + 2))
    ospec = pl.BlockSpec((1, lq, BRANCH_WIDTH), lambda bi, r, i: (bi, i, r))
    o, lse = pl.pallas_call(
        functools.partial(_dsw_kernel, dil=dil, lq=lq),
        grid=(b, dil, l // lq),
        in_specs=[qspec, kspec, vspec],
        out_specs=[ospec, ospec],
        out_shape=[jax.ShapeDtypeStruct((b, l, dil * BRANCH_WIDTH), BF16),
                   jax.ShapeDtypeStruct((b, l, dil * BRANCH_WIDTH), F32)],
        compiler_params=_cparams(("arbitrary", "arbitrary", "arbitrary")),
        name=f"dsw{dil}",
    )(cv, cv, cv)
    return o.reshape(b * l, dil * BRANCH_WIDTH), lse.reshape(b * l, dil * BRANCH_WIDTH)


def _mla_prep_kernel(cq_ref, ckv_ref, kpe_ref, kper_ref, cos_ref, sin_ref, gq_ref, wl_ref, wr_ref,
                     gkv_ref, wk_ref, wv_ref, q_ref, k_ref, v_ref):
    cos = cos_ref[...]
    sin = sin_ref[...]
    cos4 = jnp.concatenate([cos] * N_HEADS, axis=-1)
    sin4 = jnp.concatenate([sin] * N_HEADS, axis=-1)
    cq = cq_ref[...].astype(F32)
    ms = jnp.sum(cq * cq, axis=-1, keepdims=True) * (1.0 / MLA_Q_RANK)
    qn = (cq * lax.rsqrt(ms + RMS_EPS) * gq_ref[...]).astype(BF16)
    scale = (MLA_NOPE_DIM + MLA_ROPE_DIM) ** -0.5 * LOG2E
    q = (_dot(qn, wl_ref[...]) * cos4 + _dot(qn, wr_ref[...]) * sin4) * scale
    q_ref[...] = q.astype(BF16)
    ckv = ckv_ref[...].astype(F32)
    ms = jnp.mean(ckv * ckv, axis=-1, keepdims=True)
    kn = (ckv * lax.rsqrt(ms + RMS_EPS) * gkv_ref[...]).astype(BF16)
    kpe = kpe_ref[...].astype(F32) * cos + kper_ref[...].astype(F32) * sin
    k = _dot(kn, wk_ref[...]) + jnp.concatenate([kpe] * N_HEADS, axis=-1)
    k_ref[...] = k.astype(BF16)
    lane = lax.broadcasted_iota(jnp.int32, (1, N_HEADS * LANE), 1) & (LANE - 1)
    v_ref[...] = (_dot(kn, wv_ref[...]) + jnp.where(lane == HEAD_DIM, 1.0, 0.0)).astype(BF16)


def _mla_prep(c2, s, cos_t, sin_t, gq, wl, wr, gkv, wk, wv):
    n = c2.shape[0]
    tm = 512
    spb = s // tm
    hq = N_HEADS * LANE

    def cspec(off, width):
        return pl.BlockSpec((tm, width), lambda i: (i, off // width))

    tspec = pl.BlockSpec((tm, LANE), lambda i: (i % spb, 0))
    return pl.pallas_call(
        _mla_prep_kernel,
        grid=(n // tm,),
        in_specs=[cspec(C_BCQ, 256), cspec(C_BCKV, 128), cspec(C_BKPE, 128), cspec(C_BKPEROT, 128),
                  tspec, tspec,
                  _const_spec((1, 256)), _const_spec((256, hq)), _const_spec((256, hq)),
                  _const_spec((1, 128)), _const_spec((128, hq)), _const_spec((128, hq))],
        out_specs=[pl.BlockSpec((tm, hq), lambda i: (i, 0)),
                   pl.BlockSpec((tm, hq), lambda i: (i, 0)),
                   pl.BlockSpec((tm, hq), lambda i: (i, 0))],
        out_shape=[jax.ShapeDtypeStruct((n, hq), BF16),
                   jax.ShapeDtypeStruct((n, hq), BF16),
                   jax.ShapeDtypeStruct((n, hq), BF16)],
        compiler_params=_cparams(("arbitrary",)),
        name="mla_prep",
    )(c2, c2, c2, c2, cos_t, sin_t, gq, wl, wr, gkv, wk, wv)


def _mla_flash_kernel(q_ref, k_ref, v_ref, o_ref):
    i = pl.program_id(1)
    t0 = i * MLA_TQ
    row = lax.broadcasted_iota(jnp.int32, (MLA_TQ, MLA_TK), 0)
    col = lax.broadcasted_iota(jnp.int32, (MLA_TQ, MLA_TK), 1)

    def step(j, carry, masked):
        ks = pl.multiple_of(j * MLA_TK, MLA_TK)
        if masked:
            keep = (col - row) <= (t0 - ks)
        hls = [slice(h * LANE, (h + 1) * LANE) for h in range(N_HEADS)]
        ss = [_dot_nt(q_ref[0, :, hl], k_ref[0, pl.ds(ks, MLA_TK), hl]) for hl in hls]
        ps, ms = [], []
        for h in range(N_HEADS):
            s = jnp.where(keep, ss[h], NEG) if masked else ss[h]
            m_new = jnp.maximum(carry[h][0], jnp.max(s, axis=-1, keepdims=True))
            ps.append(jnp.exp2(s - m_new).astype(BF16))
            ms.append(m_new)
        new = []
        for h in range(N_HEADS):
            m, acc = carry[h]
            acc = jnp.exp2(m - ms[h]) * acc + _dot(ps[h], v_ref[0, pl.ds(ks, MLA_TK), hls[h]])
            new.append((ms[h], acc))
        return tuple(new)

    init = tuple((jnp.full((MLA_TQ, 1), NEG, F32), jnp.zeros((MLA_TQ, LANE), F32)) for _ in range(N_HEADS))
    nfull = t0 // MLA_TK
    carry = lax.fori_loop(0, nfull, functools.partial(step, masked=False), init)
    carry = step(nfull, carry, True)
    o_ref[0] = jnp.concatenate([acc[:, 0:HEAD_DIM] / acc[:, HEAD_DIM:HEAD_DIM + 1] for (_, acc) in carry],
                               axis=-1).astype(BF16)


def _mla_flash(q3, k3, v3):
    b, s, hq = q3.shape
    return pl.pallas_call(
        _mla_flash_kernel,
        grid=(b, s // MLA_TQ),
        in_specs=[pl.BlockSpec((1, MLA_TQ, hq), lambda bi, i: (bi, i, 0)),
                  pl.BlockSpec((1, s, hq), lambda bi, i: (bi, 0, 0)),
                  pl.BlockSpec((1, s, hq), lambda bi, i: (bi, 0, 0))],
        out_specs=pl.BlockSpec((1, MLA_TQ, BRANCH_WIDTH), lambda bi, i: (bi, i, 0)),
        out_shape=jax.ShapeDtypeStruct((b, s, BRANCH_WIDTH), BF16),
        compiler_params=_cparams(("arbitrary", "arbitrary")),
        name="mla_flash",
    )(q3, k3, v3)


def _nsa_cmp_kernel(ch_ref, pt_ref, pb_ref, wt_ref, wb_ref, b1_ref, w2_ref, b2_ref, o_ref):
    ch = ch_ref[0]
    top = _dot(ch, wt_ref[...])
    bot = _dot(ch, wb_ref[...])
    const = (_dot(pt_ref[...], wt_ref[...]) + _dot(pb_ref[...], wb_ref[...]))[0:1, :] + b1_ref[...]
    n = ch.shape[0]
    hid = top + pltpu.roll(bot, n - 1, 0) + const
    hid = jax.nn.gelu(hid)
    o_ref[0] = (_dot(hid.astype(BF16), w2_ref[...]) + b2_ref[...]).astype(BF16)


def _nsa_compress(kvc3, pos_t, pos_b, wt, wb, b1, w2, b2):
    b, s, _ = kvc3.shape
    nch = s // NSA_CMP_STRIDE
    cw = NSA_CMP_STRIDE * 128
    ch = kvc3.reshape(b, nch, cw)
    return pl.pallas_call(
        _nsa_cmp_kernel,
        grid=(b,),
        in_specs=[pl.BlockSpec((1, nch, cw), lambda bi: (bi, 0, 0)),
                  _const_spec((8, cw)), _const_spec((8, cw)),
                  _const_spec((cw, 2 * NSA_CMP_HIDDEN)), _const_spec((cw, 2 * NSA_CMP_HIDDEN)),
                  _const_spec((1, 2 * NSA_CMP_HIDDEN)),
                  _const_spec((2 * NSA_CMP_HIDDEN, 2 * NSA_KV_DIM)), _const_spec((1, 2 * NSA_KV_DIM))],
        out_specs=pl.BlockSpec((1, nch, 2 * NSA_KV_DIM), lambda bi: (bi, 0, 0)),
        out_shape=jax.ShapeDtypeStruct((b, nch, 2 * NSA_KV_DIM), BF16),
        compiler_params=_cparams(("arbitrary",)),
        name="nsa_compress",
    )(ch, pos_t, pos_b, wt, wb, b1, w2, b2)


def _split_dot_nt(w, x, terms=3):
    out = None
    rem = x
    for _ in range(terms):
        part = rem.astype(BF16)
        rem = rem - part.astype(F32)
        y = _dot_nt(w, part)
        out = y if out is None else out + y
    return out


def _nsa_sel_table(s):
    t = np.arange(s)
    tab = np.zeros((s, 2 * LANE), np.float32)
    tab[:, HEAD_DIM] = t >> 7
    tab[:, HEAD_DIM + 1] = t & 127
    tab[t, LANE + t // NSA_SEL_BLOCK] = NEG
    return jnp.asarray(tab, dtype=BF16)


def _nsa_cmp_table(ncmp):
    end = np.arange(ncmp) * NSA_CMP_STRIDE + NSA_CMP_BLOCK - 1
    tab = np.zeros((ncmp, LANE), np.float32)
    tab[:, HEAD_DIM] = end >> 7
    tab[:, HEAD_DIM + 1] = end & 127
    return jnp.asarray(tab, dtype=BF16)


def _nsa_kernel(q_ref, g_ref, cmp_ref, sel_ref, win_ref, tab_ref, ctab_ref, o_ref, *, ncmp):
    i = pl.program_id(1)
    t0 = i * NSA_TQ
    hr = N_HEADS * NSA_TQ
    q = q_ref[0]
    q4 = jnp.concatenate([q[:, h * HEAD_DIM:(h + 1) * HEAD_DIM] for h in range(N_HEADS)], axis=0)
    rowh = lax.broadcasted_iota(jnp.int32, (hr, 1), 0)
    hidx = rowh // NSA_TQ
    r = rowh % NSA_TQ
    slope = jnp.where(hidx == 0, SLOPES_C[0],
                      jnp.where(hidx == 1, SLOPES_C[1], jnp.where(hidx == 2, SLOPES_C[2], SLOPES_C[3]))).astype(F32)

    lane64 = lax.broadcasted_iota(jnp.int32, (hr, HEAD_DIM), 1)
    alibi_q = jnp.where(lane64 == 0, slope * 128.0, jnp.where(lane64 == 1, slope, 0.0)).astype(BF16)
    q4a = jnp.concatenate([q4, alibi_q], axis=1)
    lane = lax.broadcasted_iota(jnp.int32, (1, LANE), 1)
    keep_k = jnp.where(lane < NSA_KV_DIM, 1.0, 0.0).astype(BF16)
    keep_v = jnp.where(lane >= NSA_KV_DIM, 1.0, 0.0).astype(BF16)
    one0 = jnp.where(lane == 0, 1.0, 0.0).astype(BF16)

    def value_operand(kv):
        return kv * keep_v + one0

    def masked_softmax(s, mask):
        sb = jnp.where(mask, s, NEG)
        m = jnp.maximum(jnp.max(sb, axis=-1, keepdims=True), 0.1 * NEG)
        return jnp.exp(sb - m)

    def normalise(pv):
        return pv[:, NSA_KV_DIM:2 * NSA_KV_DIM], jnp.maximum(pv[:, 0:1], 1e-30)

    ckv = cmp_ref[0]
    ncol = lax.broadcasted_iota(jnp.int32, (1, ncmp), 1)
    last_ok = (t0 + r - (NSA_CMP_BLOCK - 1)) >> 4
    s_cmp = _dot_nt(q4a, ckv * keep_k + ctab_ref[...])
    wlen = NSA_WINDOW + NSA_TQ
    kstart = pl.multiple_of(jnp.maximum(t0 - NSA_WINDOW, 0), TILE)
    wkv = win_ref[0, pl.ds(kstart, wlen), :]
    s_win = _dot_nt(q4a, wkv * keep_k + tab_ref[pl.ds(kstart, wlen), 0:LANE])
    e_cmp = masked_softmax(s_cmp, ncol <= last_ok)
    den_c = jnp.maximum(jnp.sum(e_cmp, axis=-1, keepdims=True), 1e-30)
    p_cmp = e_cmp / den_c
    o_cmp = _dot(p_cmp.astype(BF16), ckv)[:, NSA_KV_DIM:2 * NSA_KV_DIM]
    psum = (p_cmp[0:NSA_TQ] + p_cmp[NSA_TQ:2 * NSA_TQ]
            + p_cmp[2 * NSA_TQ:3 * NSA_TQ] + p_cmp[3 * NSA_TQ:4 * NSA_TQ])
    jrow = lax.broadcasted_iota(jnp.int32, (TILE, ncmp), 0)
    ncol2 = lax.broadcasted_iota(jnp.int32, (TILE, ncmp), 1)
    delta = jrow * (NSA_SEL_BLOCK // NSA_CMP_STRIDE) - ncol2
    selmap = jnp.where((delta == 0) | (delta == 4), 1.0,
                       jnp.where((delta > 0) & (delta < 4), 2.0, 0.0)).astype(BF16)
    p_sel = _split_dot_nt(selmap, psum)

    blk = lax.broadcasted_iota(jnp.int32, (TILE, NSA_TQ), 0)
    tq = t0 + lax.broadcasted_iota(jnp.int32, (TILE, NSA_TQ), 1)
    cur = tq >> 6
    forced = (blk == 0) | (blk == cur) | (blk == cur - 1)
    cand = jnp.where((blk <= cur) & jnp.logical_not(forced), p_sel, -1.0)
    picked = jnp.where(forced, 1.0, 0.0)
    for _ in range(NSA_TOP_N - 3):
        m = jnp.max(cand, axis=0, keepdims=True)
        idx = jnp.min(jnp.where(cand == m, blk, TILE), axis=0, keepdims=True)
        hit = (blk == idx) & (m >= 0.0)
        picked = jnp.where(hit, 1.0, picked)
        cand = jnp.where(hit, -1.0, cand)
    not_sel = jnp.transpose(1.0 - picked).astype(BF16)

    q4aug = jnp.concatenate([q4a, jnp.concatenate([not_sel] * N_HEADS, axis=0)], axis=1)
    colk = lax.broadcasted_iota(jnp.int32, (hr, NSA_TK), 1)

    def sel_step(j, carry, diag):
        m, acc = carry
        ks = pl.multiple_of(j * NSA_TK, NSA_TK)
        kv = sel_ref[0, pl.ds(ks, NSA_TK), :]
        tb = tab_ref[pl.ds(ks, NSA_TK), :]
        kaug = jnp.concatenate([kv * keep_k + tb[:, 0:LANE], tb[:, LANE:2 * LANE]], axis=1)
        s = _dot_nt(q4aug, kaug)
        if diag:
            s = jnp.where((colk - r) <= (t0 - ks), s, NEG)
        m_new = jnp.maximum(m, jnp.max(s, axis=-1, keepdims=True))
        p = jnp.exp(s - m_new)
        acc = jnp.exp(m - m_new) * acc + _dot(p.astype(BF16), value_operand(kv))
        return m_new, acc

    init = (jnp.full((hr, 1), NEG, F32), jnp.zeros((hr, LANE), F32))
    jd = t0 // NSA_TK
    carry = sel_step(jd, init, True)
    _, acc = lax.fori_loop(0, jd, functools.partial(sel_step, diag=False), carry)
    o_slc, den_s = normalise(acc)
    o_slc = o_slc / den_s

    colw = lax.broadcasted_iota(jnp.int32, (hr, wlen), 1)
    dist_w = (t0 - kstart) + r - colw
    e_win = masked_softmax(s_win, (dist_w >= 0) & (dist_w < NSA_WINDOW))
    o_win, den_w = normalise(_dot(e_win.astype(BF16), value_operand(wkv)))
    o_win = o_win / den_w

    g = jax.nn.sigmoid(g_ref[0].astype(F32))

    def gate_col(kk):
        return jnp.concatenate([g[:, 3 * h + kk:3 * h + kk + 1] for h in range(N_HEADS)], axis=0)

    out4 = gate_col(0) * o_cmp + gate_col(1) * o_slc + gate_col(2) * o_win
    o_ref[0] = jnp.concatenate([out4[h * NSA_TQ:(h + 1) * NSA_TQ] for h in range(N_HEADS)], axis=-1).astype(BF16)


def _nsa_attention(c3, cmp3):
    b, s, _ = c3.shape
    ncmp = cmp3.shape[1]
    return pl.pallas_call(
        functools.partial(_nsa_kernel, ncmp=ncmp),
        grid=(b, s // NSA_TQ),
        in_specs=[pl.BlockSpec((1, NSA_TQ, BRANCH_WIDTH), lambda bi, i: (bi, i, C_CQ // BRANCH_WIDTH)),
                  pl.BlockSpec((1, NSA_TQ, 128), lambda bi, i: (bi, i, C_CG // 128)),
                  pl.BlockSpec((1, ncmp, 128), lambda bi, i: (bi, 0, 0)),
                  pl.BlockSpec((1, s, 128), lambda bi, i: (bi, 0, C_CSEL // 128)),
                  pl.BlockSpec((1, s, 128), lambda bi, i: (bi, 0, C_CWIN // 128)),
                  _const_spec((s, 2 * LANE)), _const_spec((ncmp, LANE))],
        out_specs=pl.BlockSpec((1, NSA_TQ, BRANCH_WIDTH), lambda bi, i: (bi, i, 0)),
        out_shape=jax.ShapeDtypeStruct((b, s, BRANCH_WIDTH), BF16),
        compiler_params=_cparams(("arbitrary", "arbitrary")),
        name="nsa_attn",
    )(c3, c3, cmp3, c3, c3, _nsa_sel_table(s), _nsa_cmp_table(ncmp))


def _sb_kernel(q_ref, k_ref, v_ref, o_ref, run_ref, acc_ref):
    i = pl.program_id(1)
    t0 = i * SB_TQ
    row = lax.broadcasted_iota(jnp.int32, (SB_TQ, TILE), 0)
    col = lax.broadcasted_iota(jnp.int32, (SB_TQ, TILE), 1)
    urow = lax.broadcasted_iota(jnp.int32, (2 * TILE, 2 * TILE), 0) & (TILE - 1)
    ucol = lax.broadcasted_iota(jnp.int32, (2 * TILE, 2 * TILE), 1)
    umat = jnp.where((ucol >= TILE) | (urow > ucol), 1.0, 0.0).astype(BF16)
    run_ref[...] = jnp.zeros_like(run_ref)
    acc_ref[...] = jnp.zeros_like(acc_ref)

    def tile(j, masked):
        ks = pl.multiple_of(j * TILE, TILE)
        if masked:
            strict = (col - row) < (t0 - ks)
        kt = k_ref[0, pl.ds(ks, TILE), :]
        vt = v_ref[0, pl.ds(ks, TILE), :]
        hss = [slice(h * HEAD_DIM, (h + 1) * HEAD_DIM) for h in range(N_HEADS)]
        zs = [_dot_nt(q_ref[0, :, hs], kt[:, hs]) for hs in hss]
        lss, css = [], []
        for z in zs:
            ls = jnp.minimum(z, 0.0) - jnp.log(1.0 + jnp.exp(-jnp.abs(z)))
            l1m = ls - z
            if masked:
                l1m = jnp.where(strict, l1m, 0.0)
            hi = l1m.astype(BF16)
            lo = (l1m - hi.astype(F32)).astype(BF16)
            lss.append(ls)
            css.append(_dot(jnp.concatenate([hi, lo], axis=-1), umat))
        for h in range(N_HEADS):
            run = run_ref[h]
            a = jnp.exp(lss[h] + run + css[h][:, 0:TILE])
            if masked:
                a = jnp.where(strict, a, 0.0)
            acc_ref[h] += _dot(a.astype(BF16), vt[:, hss[h]])
            run_ref[h] = run + css[h][:, TILE:2 * TILE]

    def run_max():
        return jnp.max(jnp.maximum(jnp.maximum(run_ref[0], run_ref[1]), jnp.maximum(run_ref[2], run_ref[3])))

    j_top = t0 // TILE + SB_TQ // TILE - 1
    for dj in range(SB_TQ // TILE):
        tile(j_top - dj, True)

    def cond(c):
        return (c[0] >= 0) & (c[1] > SB_UNDERFLOW)

    def body(c):
        tile(c[0], False)
        return c[0] - 1, run_max()

    lax.while_loop(cond, body, (j_top - SB_TQ // TILE, run_max()))
    o_ref[0] = jnp.concatenate([acc_ref[h] for h in range(N_HEADS)], axis=-1).astype(BF16)


def _sb_attention(c3):
    b, s, _ = c3.shape
    return pl.pallas_call(
        _sb_kernel,
        grid=(b, s // SB_TQ),
        in_specs=[pl.BlockSpec((1, SB_TQ, BRANCH_WIDTH), lambda bi, i: (bi, i, C_DQ // BRANCH_WIDTH)),
                  pl.BlockSpec((1, s, BRANCH_WIDTH), lambda bi, i: (bi, 0, C_DK // BRANCH_WIDTH)),
                  pl.BlockSpec((1, s, BRANCH_WIDTH), lambda bi, i: (bi, 0, C_DV // BRANCH_WIDTH))],
        out_specs=pl.BlockSpec((1, SB_TQ, BRANCH_WIDTH), lambda bi, i: (bi, i, 0)),
        out_shape=jax.ShapeDtypeStruct((b, s, BRANCH_WIDTH), BF16),
        scratch_shapes=[pltpu.VMEM((N_HEADS, SB_TQ, TILE), F32), pltpu.VMEM((N_HEADS, SB_TQ, HEAD_DIM), F32)],
        compiler_params=_cparams(("arbitrary", "arbitrary")),
        name="stickbreak",
    )(c3, c3, c3)


def _merge_kernel(x_ref, ng_ref, o1_ref, o2_ref, o3_ref, l1_ref, l2_ref, l3_ref, ga_ref,
                  yb_ref, gb_ref, yc_ref, gc_ref, yd_ref, gd_ref,
                  wm_ref, bm_ref, wb_ref, wo_ref, fg_ref, out_ref, *scratch, final):
    x = x_ref[...]
    ms = jnp.mean(x * x, axis=-1, keepdims=True)
    h = (x * lax.rsqrt(ms + RMS_EPS) * ng_ref[...]).astype(BF16)

    def token_order(ref, scr, dil):
        rows = x.shape[0] // dil
        for r in range(dil):
            for jl in range(BRANCH_WIDTH // LANE):
                lo = r * BRANCH_WIDTH + jl * LANE
                scr[jl, pl.ds(r, rows, stride=dil), :] = ref[:, lo:lo + LANE].astype(F32)
        return jnp.concatenate([scr[jl] for jl in range(BRANCH_WIDTH // LANE)], axis=-1)

    o2, l2 = token_order(o2_ref, scratch[0], 4), token_order(l2_ref, scratch[1], 4)
    o3, l3 = token_order(o3_ref, scratch[2], 16), token_order(l3_ref, scratch[3], 16)
    l1 = l1_ref[...]
    lm = jnp.maximum(jnp.maximum(l1, l2), l3)
    e1, e2, e3 = jnp.exp(l1 - lm), jnp.exp(l2 - lm), jnp.exp(l3 - lm)
    ya = (e1 * o1_ref[...].astype(F32) + e2 * o2 + e3 * o3) / (e1 + e2 + e3)
    ys = (ya, yb_ref[...].astype(F32), yc_ref[...].astype(F32), yd_ref[...].astype(F32))
    gs = (ga_ref, gb_ref, gc_ref, gd_ref)
    merged = None
    for i in range(4):
        y = (ys[i] * jax.nn.silu(gs[i][...].astype(F32))).astype(BF16)
        gate = jax.nn.sigmoid(_dot(h, wm_ref[i]) + bm_ref[i])
        term = gate * _dot(y, wb_ref[i])
        merged = term if merged is None else merged + term
    xn = x + _dot(merged.astype(BF16), wo_ref[...])
    if final:
        ms = jnp.mean(xn * xn, axis=-1, keepdims=True)
        xn = xn * lax.rsqrt(ms + RMS_EPS) * fg_ref[...]
    out_ref[...] = xn


def _merge(x2, ng, a_outs, a_lses, c2, yb, yc, yd, wm, bm, wb, wo, fg, final):
    n, d = x2.shape
    tm = 256
    bw = BRANCH_WIDTH

    def rows(width):
        return pl.BlockSpec((tm, width), lambda i: (i, 0))

    def cspec(off):
        return pl.BlockSpec((tm, bw), lambda i: (i, off // bw))

    def dspec(dil):
        return pl.BlockSpec((tm // dil, dil * bw), lambda i: (i, 0))

    return pl.pallas_call(
        functools.partial(_merge_kernel, final=final),
        grid=(n // tm,),
        in_specs=[rows(d), _const_spec((1, d)),
                  dspec(1), dspec(4), dspec(16), dspec(1), dspec(4), dspec(16), cspec(C_AGATE),
                  rows(bw), cspec(C_BGATE), rows(bw), cspec(C_CGATE), rows(bw), cspec(C_DGATE),
                  _const_spec((4, d, d)), _const_spec((4, 1, d)), _const_spec((4, bw, d)),
                  _const_spec((d, d)), _const_spec((1, d))],
        out_specs=rows(d),
        out_shape=jax.ShapeDtypeStruct((n, d), F32),
        scratch_shapes=[pltpu.VMEM((bw // LANE, tm, LANE), F32) for _ in range(4)],
        compiler_params=_cparams(("arbitrary",)),
        name="merge",
    )(x2, ng, *a_outs, *a_lses, c2, yb, c2, yc, c2, yd, c2, wm, bm, wb, wo, fg)


def _split_w_in(w):
    widths = (256, 256, 256, 256, MLA_Q_RANK, MLA_KV_RANK, MLA_ROPE_DIM, 256,
              256, 64, 64, 64, 64, 64, 64, 3 * N_HEADS, 256, 256, 256, 256, 256)
    names = ("a_q", "a_k", "a_v", "a_gate", "b_cq", "b_ckv", "b_kpe", "b_gate",
             "c_q", "c_kc", "c_vc", "c_ks", "c_vs", "c_kw", "c_vw", "c_g", "c_gate",
             "d_q", "d_k", "d_v", "d_gate")
    out, off = {}, 0
    for nm, wd in zip(names, widths):
        out[nm] = w[:, off:off + wd]
        off += wd
    return out


def _rot_cols(w):
    half = w.shape[1] // 2
    return jnp.concatenate([-w[:, half:], w[:, :half]], axis=1)


def _prep_w_in(w):
    p = _split_w_in(w)
    d = w.shape[0]
    qs = HEAD_DIM ** -0.5

    def z(n):
        return jnp.zeros((d, n), w.dtype)

    kpe_blk = jnp.concatenate([z(ROPE_LANE), p["b_kpe"], z(LANE - ROPE_LANE - MLA_ROPE_DIM)], axis=1)
    kper_blk = jnp.concatenate([z(ROPE_LANE), _rot_cols(p["b_kpe"]), z(LANE - ROPE_LANE - MLA_ROPE_DIM)], axis=1)
    cols = [p["a_q"] * qs, p["a_k"], p["a_v"],
            p["a_gate"], p["b_cq"], z(256 - MLA_Q_RANK), p["b_ckv"], kpe_blk, kper_blk,
            p["c_g"], z(128 - 3 * N_HEADS), p["b_gate"],
            p["c_q"] * qs, p["c_ks"], p["c_vs"], p["c_kw"], p["c_vw"], p["c_gate"],
            p["d_q"] * qs, p["d_k"], p["d_v"], p["d_gate"],
            p["c_kc"], p["c_vc"]]
    out = jnp.concatenate(cols, axis=1)
    assert out.shape[1] == W_WIDTH
    return out.astype(BF16)


def _prep_mla(w_uq, w_ukv, gq):
    qd = MLA_NOPE_DIM + MLA_ROPE_DIM
    pad = LANE - qd
    wl, wr, wk, wv = [], [], [], []
    for h in range(N_HEADS):
        nope = w_uq[:, h * qd:h * qd + MLA_NOPE_DIM]
        rp = w_uq[:, h * qd + MLA_NOPE_DIM:(h + 1) * qd]
        zq = jnp.zeros((MLA_Q_RANK, pad), w_uq.dtype)
        wl += [nope, rp, zq]
        wr += [jnp.zeros_like(nope), _rot_cols(rp), zq]
        kn = w_ukv[:, h * 128:h * 128 + MLA_NOPE_DIM]
        wk += [kn, jnp.zeros((MLA_KV_RANK, LANE - MLA_NOPE_DIM), w_ukv.dtype)]
        wv += [w_ukv[:, h * 128 + MLA_NOPE_DIM:(h + 1) * 128],
               jnp.zeros((MLA_KV_RANK, LANE - HEAD_DIM), w_ukv.dtype)]
    rpad = ((0, 256 - MLA_Q_RANK), (0, 0))
    wl = jnp.pad(jnp.concatenate(wl, axis=1), rpad).astype(BF16)
    wr = jnp.pad(jnp.concatenate(wr, axis=1), rpad).astype(BF16)
    gq = jnp.pad(gq, (0, 256 - MLA_Q_RANK)).reshape(1, 256)
    return wl, wr, jnp.concatenate(wk, axis=1).astype(BF16), jnp.concatenate(wv, axis=1).astype(BF16), gq


def _rope_tables(s):
    half = MLA_ROPE_DIM // 2
    inv = ROPE_THETA ** (-jnp.arange(half, dtype=F32) / half)
    ang = jnp.arange(s, dtype=F32)[:, None] * inv[None, :]
    cos, sin = jnp.cos(ang), jnp.sin(ang)
    ones = jnp.ones((s, ROPE_LANE), F32)
    zl = jnp.zeros((s, ROPE_LANE), F32)
    zr = jnp.zeros((s, LANE - ROPE_LANE - MLA_ROPE_DIM), F32)
    return (jnp.concatenate([ones, cos, cos, zr], axis=1), jnp.concatenate([zl, sin, sin, zr], axis=1))


def _prep_nsa(pos, w1, b1, w2, b2):
    half = NSA_CMP_BLOCK // 2
    eye = jnp.eye(2, dtype=w1.dtype)
    w1r = w1.reshape(2, NSA_CMP_BLOCK, NSA_KV_DIM, NSA_CMP_HIDDEN)
    posr = pos

    def comb_w(part):
        return jnp.einsum("ktdh,kj->tkdjh", part, eye).reshape(half * 2 * NSA_KV_DIM, 2 * NSA_CMP_HIDDEN)

    def comb_p(part):
        flat = jnp.transpose(part, (1, 0, 2)).reshape(1, half * 2 * NSA_KV_DIM)
        return jnp.pad(flat, ((0, 7), (0, 0))).astype(BF16)

    wt, wb = comb_w(w1r[:, :half]).astype(BF16), comb_w(w1r[:, half:]).astype(BF16)
    pt, pb = comb_p(posr[:, :half]), comb_p(posr[:, half:])
    b1c = b1.reshape(1, 2 * NSA_CMP_HIDDEN)
    zero = jnp.zeros((NSA_CMP_HIDDEN, NSA_KV_DIM), w2.dtype)
    w2c = jnp.concatenate([jnp.concatenate([w2[0], zero], axis=1),
                           jnp.concatenate([zero, w2[1]], axis=1)], axis=0).astype(BF16)
    b2c = b2.reshape(1, 2 * NSA_KV_DIM)
    return pt, pb, wt, wb, b1c, w2c, b2c


def kernel(x, norm_g, w_in, mla_q_norm, mla_w_uq, mla_kv_norm, mla_w_ukv, nsa_pos, nsa_w1, nsa_b1,
           nsa_w2, nsa_b2, w_branch, w_merge, b_merge, w_out, final_norm_g):
    b, s, d = x.shape
    depth = norm_g.shape[0]
    n = b * s
    assert s % (TILE * max(DSW_DILATIONS)) == 0 and s >= NSA_WINDOW + TILE
    cos_t, sin_t = _rope_tables(s)
    x2 = x.reshape(n, d)
    for layer in range(depth):
        *a_cls, c2, kvc = _inproj(x2, norm_g[layer].reshape(1, d), _prep_w_in(w_in[layer]))
        c3 = c2.reshape(b, s, C_WIDTH)
        a_res = [_dsw_attention(a.reshape(b, s // dil, dil * A_WIDTH), dil) for a, dil in zip(a_cls, DSW_DILATIONS)]
        wl, wr, wk, wv, gq = _prep_mla(mla_w_uq[layer], mla_w_ukv[layer], mla_q_norm[layer])
        qb, kb, vb = _mla_prep(c2, s, cos_t, sin_t, gq, wl, wr, mla_kv_norm[layer].reshape(1, MLA_KV_RANK), wk, wv)
        hq = N_HEADS * LANE
        yb = _mla_flash(qb.reshape(b, s, hq), kb.reshape(b, s, hq), vb.reshape(b, s, hq))
        cmp3 = _nsa_compress(kvc.reshape(b, s, 128),
                             *_prep_nsa(nsa_pos[layer], nsa_w1[layer], nsa_b1[layer], nsa_w2[layer], nsa_b2[layer]))
        yc = _nsa_attention(c3, cmp3)
        yd = _sb_attention(c3)
        x2 = _merge(x2, norm_g[layer].reshape(1, d), [r[0] for r in a_res], [r[1] for r in a_res], c2,
                    yb.reshape(n, BRANCH_WIDTH), yc.reshape(n, BRANCH_WIDTH), yd.reshape(n, BRANCH_WIDTH),
                    w_merge[layer].astype(BF16), b_merge[layer].reshape(4, 1, d),
                    w_branch[layer].astype(BF16), w_out[layer].astype(BF16),
                    final_norm_g.reshape(1, d), final=(layer == depth - 1))
    return x2.reshape(b, s, d)
```

```python
import functools

import numpy as np
import jax
import jax.numpy as jnp
from jax import lax
from jax.experimental import pallas as pl
from jax.experimental.pallas import tpu as pltpu

F32 = jnp.float32
BF16 = jnp.bfloat16

HEAD_DIM = 64
N_HEADS = 4
BRANCH_WIDTH = N_HEADS * HEAD_DIM
RMS_EPS = 1e-6
DSW_DILATIONS = (1, 4, 16)
DSW_SPAN = 128
MLA_Q_RANK = 192
MLA_KV_RANK = 128
MLA_NOPE_DIM = 64
MLA_ROPE_DIM = 32
ROPE_THETA = 10000.0
NSA_CMP_BLOCK = 32
NSA_CMP_STRIDE = 16
NSA_CMP_HIDDEN = 128
NSA_SEL_BLOCK = 64
NSA_TOP_N = 16
NSA_WINDOW = 512
NSA_KV_DIM = 64
_ALIBI = [2.0 ** (-(i + 1)) for i in range(8)]
SLOPES_A = _ALIBI[0::2]
SLOPES_C = _ALIBI[1::2]

LANE = 128
TILE = 128
VMEM_LIMIT = 56 * 1024 * 1024
NEG = -1e30
LOG2E = 1.4426950408889634
MLA_TQ, MLA_TK = 512, 1024
SB_TQ = 256
SB_UNDERFLOW = -104.0

A_WIDTH = 3 * 256
C_AGATE = 0
C_BCQ, C_BCKV, C_BKPE, C_BKPEROT = 256, 512, 640, 768
C_CG, C_BGATE, C_CQ, C_CSEL, C_CWIN, C_CGATE = 896, 1024, 1280, 1536, 1664, 1792
C_DQ, C_DK, C_DV, C_DGATE = 2048, 2304, 2560, 2816
C_WIDTH = 3072
W_C = A_WIDTH
W_KVC = A_WIDTH + C_WIDTH
W_WIDTH = W_KVC + 128
NSA_TK = 1024
NSA_TQ = 256
ROPE_LANE = 64


def _dot(a, b):
    return jnp.dot(a, b, preferred_element_type=F32)


def _dot_nt(a, b):
    return lax.dot_general(a, b, (((1,), (1,)), ((), ())), preferred_element_type=F32)


def _cparams(sem):
    return pltpu.CompilerParams(dimension_semantics=sem, vmem_limit_bytes=VMEM_LIMIT)


def _const_spec(shape):
    nd = len(shape)
    return pl.BlockSpec(shape, lambda *_: (0,) * nd)


def _inproj_kernel(x_ref, g_ref, w_ref, a1_ref, a4_ref, a16_ref, c_ref, kvc_ref, a_scr):
    x = x_ref[...]
    ms = jnp.mean(x * x, axis=-1, keepdims=True)
    h = (x * lax.rsqrt(ms + RMS_EPS) * g_ref[...]).astype(BF16)
    cw = 256
    for j in range(A_WIDTH // cw):
        y = _dot(h, w_ref[:, j * cw:(j + 1) * cw])
        a1_ref[:, j * cw:(j + 1) * cw] = y.astype(BF16)
        for jj in range(cw // LANE):
            a_scr[(j * cw) // LANE + jj] = y[:, jj * LANE:(jj + 1) * LANE]
    tm = a_scr.shape[1]
    for dil, ref in ((4, a4_ref), (16, a16_ref)):
        for r in range(dil):
            for jl in range(A_WIDTH // LANE):
                rows = a_scr[jl, pl.ds(r, tm // dil, stride=dil), :]
                ref[:, r * A_WIDTH + jl * LANE:r * A_WIDTH + (jl + 1) * LANE] = rows.astype(BF16)
    for j in range(C_WIDTH // cw):
        c_ref[:, j * cw:(j + 1) * cw] = _dot(h, w_ref[:, W_C + j * cw:W_C + (j + 1) * cw]).astype(BF16)
    kvc_ref[...] = _dot(h, w_ref[:, W_KVC:W_KVC + 128]).astype(BF16)


def _inproj(x2, g, w):
    n, d = x2.shape
    tm = 512
    return pl.pallas_call(
        _inproj_kernel,
        grid=(n // tm,),
        in_specs=[pl.BlockSpec((tm, d), lambda i: (i, 0)),
                  _const_spec((1, d)),
                  _const_spec((d, W_WIDTH))],
        out_specs=[pl.BlockSpec((tm // dil, dil * A_WIDTH), lambda i: (i, 0)) for dil in DSW_DILATIONS]
        + [pl.BlockSpec((tm, C_WIDTH), lambda i: (i, 0)),
           pl.BlockSpec((tm, 128), lambda i: (i, 0))],
        out_shape=[jax.ShapeDtypeStruct((n // dil, dil * A_WIDTH), BF16) for dil in DSW_DILATIONS]
        + [jax.ShapeDtypeStruct((n, C_WIDTH), BF16),
           jax.ShapeDtypeStruct((n, 128), BF16)],
        scratch_shapes=[pltpu.VMEM((A_WIDTH // LANE, tm, LANE), F32)],
        compiler_params=_cparams(("arbitrary",)),
        name="inproj",
    )(x2, g, w)


def _dsw_kernel(q_ref, k_ref, v_ref, o_ref, lse_ref, *, dil, lq):
    blk = pl.program_id(2)
    row = lax.broadcasted_iota(jnp.int32, (TILE, 2 * TILE), 0)
    col = lax.broadcasted_iota(jnp.int32, (TILE, 2 * TILE), 1)
    hss = [slice(h * HEAD_DIM, (h + 1) * HEAD_DIM) for h in range(N_HEADS)]
    nt = lq // TILE

    def operands(t):
        u0 = blk * lq + t * TILE
        kstart = pl.multiple_of(jnp.maximum(u0 - TILE, 0), TILE)
        d = (u0 - kstart) + row - col
        valid = (d >= 0) & (d <= DSW_SPAN)
        return (q_ref[0, t * TILE:(t + 1) * TILE, :], k_ref[0, pl.ds(kstart, 2 * TILE), :],
                v_ref[0, pl.ds(kstart, 2 * TILE), :], valid, d.astype(F32))

    def softmax(s, h, valid, df):
        s = jnp.where(valid, s - (SLOPES_A[h] * dil) * df, NEG)
        m = jnp.max(s, axis=-1, keepdims=True)
        p = jnp.exp(s - m)
        l = jnp.sum(p, axis=-1, keepdims=True)
        return p.astype(BF16), l, jnp.broadcast_to(m + jnp.log(l), (TILE, HEAD_DIM))

    ops = [operands(t) for t in range(nt)]
    ss = [[_dot_nt(ops[t][0][:, hs], ops[t][1][:, hs]) for hs in hss] for t in range(nt)]
    sm = [[softmax(ss[t][h], h, ops[t][3], ops[t][4]) for h in range(N_HEADS)] for t in range(nt)]
    for t in range(nt):
        outs = [_dot(sm[t][h][0], ops[t][2][:, hss[h]]) / sm[t][h][1] for h in range(N_HEADS)]
        o_ref[0, t * TILE:(t + 1) * TILE, :] = jnp.concatenate(outs, axis=-1).astype(BF16)
        lse_ref[0, t * TILE:(t + 1) * TILE, :] = jnp.concatenate([sm[t][h][2] for h in range(N_HEADS)], axis=-1)


def _dsw_attention(cv, dil):
    b, l, _ = cv.shape
    lq = min(512, l)
    ncb = A_WIDTH // BRANCH_WIDTH
    qspec = pl.BlockSpec((1, lq, BRANCH_WIDTH), lambda bi, r, i: (bi, i, r * ncb))
    kspec = pl.BlockSpec((1, l, BRANCH_WIDTH), lambda bi, r, i: (bi, 0, r * ncb + 1))
    vspec = pl.BlockSpec((1, l, BRANCH_WIDTH), lambda bi, r, i: (bi, 0, r * ncb + 2))
    ospec = pl.BlockSpec((1, lq, BRANCH_WIDTH), lambda bi, r, i: (bi, i, r))
    o, lse = pl.pallas_call(
        functools.partial(_dsw_kernel, dil=dil, lq=lq),
        grid=(b, dil, l // lq),
        in_specs=[qspec, kspec, vspec],
        out_specs=[ospec, ospec],
        out_shape=[jax.ShapeDtypeStruct((b, l, dil * BRANCH_WIDTH), BF16),
                   jax.ShapeDtypeStruct((b, l, dil * BRANCH_WIDTH), F32)],
        compiler_params=_cparams(("arbitrary", "arbitrary", "arbitrary")),
        name=f"dsw{dil}",
    )(cv, cv, cv)
    return o.reshape(b * l, dil * BRANCH_WIDTH), lse.reshape(b * l, dil * BRANCH_WIDTH)


def _mla_prep_kernel(cq_ref, ckv_ref, kpe_ref, kper_ref, cos_ref, sin_ref, gq_ref, wl_ref, wr_ref,
                     gkv_ref, wk_ref, wv_ref, q_ref, k_ref, v_ref):
    cos = cos_ref[...]
    sin = sin_ref[...]
    cos4 = jnp.concatenate([cos] * N_HEADS, axis=-1)
    sin4 = jnp.concatenate([sin] * N_HEADS, axis=-1)
    cq = cq_ref[...].astype(F32)
    ms = jnp.sum(cq * cq, axis=-1, keepdims=True) * (1.0 / MLA_Q_RANK)
    qn = (cq * lax.rsqrt(ms + RMS_EPS) * gq_ref[...]).astype(BF16)
    scale = (MLA_NOPE_DIM + MLA_ROPE_DIM) ** -0.5 * LOG2E
    q = (_dot(qn, wl_ref[...]) * cos4 + _dot(qn, wr_ref[...]) * sin4) * scale
    q_ref[...] = q.astype(BF16)
    ckv = ckv_ref[...].astype(F32)
    ms = jnp.mean(ckv * ckv, axis=-1, keepdims=True)
    kn = (ckv * lax.rsqrt(ms + RMS_EPS) * gkv_ref[...]).astype(BF16)
    kpe = kpe_ref[...].astype(F32) * cos + kper_ref[...].astype(F32) * sin
    k = _dot(kn, wk_ref[...]) + jnp.concatenate([kpe] * N_HEADS, axis=-1)
    k_ref[...] = k.astype(BF16)
    lane = lax.broadcasted_iota(jnp.int32, (1, N_HEADS * LANE), 1) & (LANE - 1)
    v_ref[...] = (_dot(kn, wv_ref[...]) + jnp.where(lane == HEAD_DIM, 1.0, 0.0)).astype(BF16)


def _mla_prep(c2, s, cos_t, sin_t, gq, wl, wr, gkv, wk, wv):
    n = c2.shape[0]
    tm = 512
    spb = s // tm
    hq = N_HEADS * LANE

    def cspec(off, width):
        return pl.BlockSpec((tm, width), lambda i: (i, off // width))

    tspec = pl.BlockSpec((tm, LANE), lambda i: (i % spb, 0))
    return pl.pallas_call(
        _mla_prep_kernel,
        grid=(n // tm,),
        in_specs=[cspec(C_BCQ, 256), cspec(C_BCKV, 128), cspec(C_BKPE, 128), cspec(C_BKPEROT, 128),
                  tspec, tspec,
                  _const_spec((1, 256)), _const_spec((256, hq)), _const_spec((256, hq)),
                  _const_spec((1, 128)), _const_spec((128, hq)), _const_spec((128, hq))],
        out_specs=[pl.BlockSpec((tm, hq), lambda i: (i, 0)),
                   pl.BlockSpec((tm, hq), lambda i: (i, 0)),
                   pl.BlockSpec((tm, hq), lambda i: (i, 0))],
        out_shape=[jax.ShapeDtypeStruct((n, hq), BF16),
                   jax.ShapeDtypeStruct((n, hq), BF16),
                   jax.ShapeDtypeStruct((n, hq), BF16)],
        compiler_params=_cparams(("arbitrary",)),
        name="mla_prep",
    )(c2, c2, c2, c2, cos_t, sin_t, gq, wl, wr, gkv, wk, wv)


def _mla_flash_kernel(q_ref, k_ref, v_ref, o_ref):
    i = pl.program_id(1)
    t0 = i * MLA_TQ
    row = lax.broadcasted_iota(jnp.int32, (MLA_TQ, MLA_TK), 0)
    col = lax.broadcasted_iota(jnp.int32, (MLA_TQ, MLA_TK), 1)

    def step(j, carry, masked):
        ks = pl.multiple_of(j * MLA_TK, MLA_TK)
        if masked:
            keep = (col - row) <= (t0 - ks)
        hls = [slice(h * LANE, (h + 1) * LANE) for h in range(N_HEADS)]
        ss = [_dot_nt(q_ref[0, :, hl], k_ref[0, pl.ds(ks, MLA_TK), hl]) for hl in hls]
        ps, ms = [], []
        for h in range(N_HEADS):
            s = jnp.where(keep, ss[h], NEG) if masked else ss[h]
            m_new = jnp.maximum(carry[h][0], jnp.max(s, axis=-1, keepdims=True))
            ps.append(jnp.exp2(s - m_new).astype(BF16))
            ms.append(m_new)
        new = []
        for h in range(N_HEADS):
            m, acc = carry[h]
            acc = jnp.exp2(m - ms[h]) * acc + _dot(ps[h], v_ref[0, pl.ds(ks, MLA_TK), hls[h]])
            new.append((ms[h], acc))
        return tuple(new)

    init = tuple((jnp.full((MLA_TQ, 1), NEG, F32), jnp.zeros((MLA_TQ, LANE), F32)) for _ in range(N_HEADS))
    nfull = t0 // MLA_TK
    carry = lax.fori_loop(0, nfull, functools.partial(step, masked=False), init)
    carry = step(nfull, carry, True)
    o_ref[0] = jnp.concatenate([acc[:, 0:HEAD_DIM] / acc[:, HEAD_DIM:HEAD_DIM + 1] for (_, acc) in carry],
                               axis=-1).astype(BF16)


def _mla_flash(q3, k3, v3):
    b, s, hq = q3.shape
    return pl.pallas_call(
        _mla_flash_kernel,
        grid=(b, s // MLA_TQ),
        in_specs=[pl.BlockSpec((1, MLA_TQ, hq), lambda bi, i: (bi, i, 0)),
                  pl.BlockSpec((1, s, hq), lambda bi, i: (bi, 0, 0)),
                  pl.BlockSpec((1, s, hq), lambda bi, i: (bi, 0, 0))],
        out_specs=pl.BlockSpec((1, MLA_TQ, BRANCH_WIDTH), lambda bi, i: (bi, i, 0)),
        out_shape=jax.ShapeDtypeStruct((b, s, BRANCH_WIDTH), BF16),
        compiler_params=_cparams(("arbitrary", "arbitrary")),
        name="mla_flash",
    )(q3, k3, v3)


def _nsa_cmp_kernel(ch_ref, pt_ref, pb_ref, wt_ref, wb_ref, b1_ref, w2_ref, b2_ref, o_ref):
    ch = ch_ref[0]
    top = _dot(ch, wt_ref[...])
    bot = _dot(ch, wb_ref[...])
    const = (_dot(pt_ref[...], wt_ref[...]) + _dot(pb_ref[...], wb_ref[...]))[0:1, :] + b1_ref[...]
    n = ch.shape[0]
    hid = top + pltpu.roll(bot, n - 1, 0) + const
    hid = jax.nn.gelu(hid)
    o_ref[0] = (_dot(hid.astype(BF16), w2_ref[...]) + b2_ref[...]).astype(BF16)


def _nsa_compress(kvc3, pos_t, pos_b, wt, wb, b1, w2, b2):
    b, s, _ = kvc3.shape
    nch = s // NSA_CMP_STRIDE
    cw = NSA_CMP_STRIDE * 128
    ch = kvc3.reshape(b, nch, cw)
    return pl.pallas_call(
        _nsa_cmp_kernel,
        grid=(b,),
        in_specs=[pl.BlockSpec((1, nch, cw), lambda bi: (bi, 0, 0)),
                  _const_spec((8, cw)), _const_spec((8, cw)),
                  _const_spec((cw, 2 * NSA_CMP_HIDDEN)), _const_spec((cw, 2 * NSA_CMP_HIDDEN)),
                  _const_spec((1, 2 * NSA_CMP_HIDDEN)),
                  _const_spec((2 * NSA_CMP_HIDDEN, 2 * NSA_KV_DIM)), _const_spec((1, 2 * NSA_KV_DIM))],
        out_specs=pl.BlockSpec((1, nch, 2 * NSA_KV_DIM), lambda bi: (bi, 0, 0)),
        out_shape=jax.ShapeDtypeStruct((b, nch, 2 * NSA_KV_DIM), BF16),
        compiler_params=_cparams(("arbitrary",)),
        name="nsa_compress",
    )(ch, pos_t, pos_b, wt, wb, b1, w2, b2)


def _split_dot_nt(w, x, terms=3):
    out = None
    rem = x
    for _ in range(terms):
        part = rem.astype(BF16)
        rem = rem - part.astype(F32)
        y = _dot_nt(w, part)
        out = y if out is None else out + y
    return out


def _nsa_sel_table(s):
    t = np.arange(s)
    tab = np.zeros((s, 2 * LANE), np.float32)
    tab[:, HEAD_DIM] = t >> 7
    tab[:, HEAD_DIM + 1] = t & 127
    tab[t, LANE + t // NSA_SEL_BLOCK] = NEG
    return jnp.asarray(tab, dtype=BF16)


def _nsa_cmp_table(ncmp):
    end = np.arange(ncmp) * NSA_CMP_STRIDE + NSA_CMP_BLOCK - 1
    tab = np.zeros((ncmp, LANE), np.float32)
    tab[:, HEAD_DIM] = end >> 7
    tab[:, HEAD_DIM + 1] = end & 127
    return jnp.asarray(tab, dtype=BF16)


def _nsa_kernel(q_ref, g_ref, cmp_ref, sel_ref, win_ref, tab_ref, ctab_ref, o_ref, *, ncmp):
    i = pl.program_id(1)
    t0 = i * NSA_TQ
    hr = N_HEADS * NSA_TQ
    q = q_ref[0]
    q4 = jnp.concatenate([q[:, h * HEAD_DIM:(h + 1) * HEAD_DIM] for h in range(N_HEADS)], axis=0)
    rowh = lax.broadcasted_iota(jnp.int32, (hr, 1), 0)
    hidx = rowh // NSA_TQ
    r = rowh % NSA_TQ
    slope = jnp.where(hidx == 0, SLOPES_C[0],
                      jnp.where(hidx == 1, SLOPES_C[1], jnp.where(hidx == 2, SLOPES_C[2], SLOPES_C[3]))).astype(F32)

    lane64 = lax.broadcasted_iota(jnp.int32, (hr, HEAD_DIM), 1)
    alibi_q = jnp.where(lane64 == 0, slope * 128.0, jnp.where(lane64 == 1, slope, 0.0)).astype(BF16)
    q4a = jnp.concatenate([q4, alibi_q], axis=1)
    lane = lax.broadcasted_iota(jnp.int32, (1, LANE), 1)
    keep_k = jnp.where(lane < NSA_KV_DIM, 1.0, 0.0).astype(BF16)
    keep_v = jnp.where(lane >= NSA_KV_DIM, 1.0, 0.0).astype(BF16)
    one0 = jnp.where(lane == 0, 1.0, 0.0).astype(BF16)

    def value_operand(kv):
        return kv * keep_v + one0

    def masked_softmax(s, mask):
        sb = jnp.where(mask, s, NEG)
        m = jnp.maximum(jnp.max(sb, axis=-1, keepdims=True), 0.1 * NEG)
        return jnp.exp(sb - m)

    def normalise(pv):
        return pv[:, NSA_KV_DIM:2 * NSA_KV_DIM], jnp.maximum(pv[:, 0:1], 1e-30)

    ckv = cmp_ref[0]
    ncol = lax.broadcasted_iota(jnp.int32, (1, ncmp), 1)
    last_ok = (t0 + r - (NSA_CMP_BLOCK - 1)) >> 4
    s_cmp = _dot_nt(q4a, ckv * keep_k + ctab_ref[...])
    wlen = NSA_WINDOW + NSA_TQ
    kstart = pl.multiple_of(jnp.maximum(t0 - NSA_WINDOW, 0), TILE)
    wkv = win_ref[0, pl.ds(kstart, wlen), :]
    s_win = _dot_nt(q4a, wkv * keep_k + tab_ref[pl.ds(kstart, wlen), 0:LANE])
    e_cmp = masked_softmax(s_cmp, ncol <= last_ok)
    den_c = jnp.maximum(jnp.sum(e_cmp, axis=-1, keepdims=True), 1e-30)
    p_cmp = e_cmp / den_c
    o_cmp = _dot(p_cmp.astype(BF16), ckv)[:, NSA_KV_DIM:2 * NSA_KV_DIM]
    psum = (p_cmp[0:NSA_TQ] + p_cmp[NSA_TQ:2 * NSA_TQ]
            + p_cmp[2 * NSA_TQ:3 * NSA_TQ] + p_cmp[3 * NSA_TQ:4 * NSA_TQ])
    jrow = lax.broadcasted_iota(jnp.int32, (TILE, ncmp), 0)
    ncol2 = lax.broadcasted_iota(jnp.int32, (TILE, ncmp), 1)
    delta = jrow * (NSA_SEL_BLOCK // NSA_CMP_STRIDE) - ncol2
    selmap = jnp.where((delta == 0) | (delta == 4), 1.0,
                       jnp.where((delta > 0) & (delta < 4), 2.0, 0.0)).astype(BF16)
    p_sel = _split_dot_nt(selmap, psum)

    blk = lax.broadcasted_iota(jnp.int32, (TILE, NSA_TQ), 0)
    tq = t0 + lax.broadcasted_iota(jnp.int32, (TILE, NSA_TQ), 1)
    cur = tq >> 6
    forced = (blk == 0) | (blk == cur) | (blk == cur - 1)
    cand = jnp.where((blk <= cur) & jnp.logical_not(forced), p_sel, -1.0)
    picked = jnp.where(forced, 1.0, 0.0)
    for _ in range(NSA_TOP_N - 3):
        m = jnp.max(cand, axis=0, keepdims=True)
        idx = jnp.min(jnp.where(cand == m, blk, TILE), axis=0, keepdims=True)
        hit = (blk == idx) & (m >= 0.0)
        picked = jnp.where(hit, 1.0, picked)
        cand = jnp.where(hit, -1.0, cand)
    not_sel = jnp.transpose(1.0 - picked).astype(BF16)

    q4aug = jnp.concatenate([q4a, jnp.concatenate([not_sel] * N_HEADS, axis=0)], axis=1)
    colk = lax.broadcasted_iota(jnp.int32, (hr, NSA_TK), 1)

    def sel_step(j, carry, diag):
        m, acc = carry
        ks = pl.multiple_of(j * NSA_TK, NSA_TK)
        kv = sel_ref[0, pl.ds(ks, NSA_TK), :]
        tb = tab_ref[pl.ds(ks, NSA_TK), :]
        kaug = jnp.concatenate([kv * keep_k + tb[:, 0:LANE], tb[:, LANE:2 * LANE]], axis=1)
        s = _dot_nt(q4aug, kaug)
        if diag:
            s = jnp.where((colk - r) <= (t0 - ks), s, NEG)
        m_new = jnp.maximum(m, jnp.max(s, axis=-1, keepdims=True))
        p = jnp.exp(s - m_new)
        acc = jnp.exp(m - m_new) * acc + _dot(p.astype(BF16), value_operand(kv))
        return m_new, acc

    init = (jnp.full((hr, 1), NEG, F32), jnp.zeros((hr, LANE), F32))
    jd = t0 // NSA_TK
    carry = sel_step(jd, init, True)
    _, acc = lax.fori_loop(0, jd, functools.partial(sel_step, diag=False), carry)
    o_slc, den_s = normalise(acc)
    o_slc = o_slc / den_s

    colw = lax.broadcasted_iota(jnp.int32, (hr, wlen), 1)
    dist_w = (t0 - kstart) + r - colw
    e_win = masked_softmax(s_win, (dist_w >= 0) & (dist_w < NSA_WINDOW))
    o_win, den_w = normalise(_dot(e_win.astype(BF16), value_operand(wkv)))
    o_win = o_win / den_w

    g = jax.nn.sigmoid(g_ref[0].astype(F32))

    def gate_col(kk):
        return jnp.concatenate([g[:, 3 * h + kk:3 * h + kk + 1] for h in range(N_HEADS)], axis=0)

    out4 = gate_col(0) * o_cmp + gate_col(1) * o_slc + gate_col(2) * o_win
    o_ref[0] = jnp.concatenate([out4[h * NSA_TQ:(h + 1) * NSA_TQ] for h in range(N_HEADS)], axis=-1).astype(BF16)


def _nsa_attention(c3, cmp3):
    b, s, _ = c3.shape
    ncmp = cmp3.shape[1]
    return pl.pallas_call(
        functools.partial(_nsa_kernel, ncmp=ncmp),
        grid=(b, s // NSA_TQ),
        in_specs=[pl.BlockSpec((1, NSA_TQ, BRANCH_WIDTH), lambda bi, i: (bi, i, C_CQ // BRANCH_WIDTH)),
                  pl.BlockSpec((1, NSA_TQ, 128), lambda bi, i: (bi, i, C_CG // 128)),
                  pl.BlockSpec((1, ncmp, 128), lambda bi, i: (bi, 0, 0)),
                  pl.BlockSpec((1, s, 128), lambda bi, i: (bi, 0, C_CSEL // 128)),
                  pl.BlockSpec((1, s, 128), lambda bi, i: (bi, 0, C_CWIN // 128)),
                  _const_spec((s, 2 * LANE)), _const_spec((ncmp, LANE))],
        out_specs=pl.BlockSpec((1, NSA_TQ, BRANCH_WIDTH), lambda bi, i: (bi, i, 0)),
        out_shape=jax.ShapeDtypeStruct((b, s, BRANCH_WIDTH), BF16),
        compiler_params=_cparams(("arbitrary", "arbitrary")),
        name="nsa_attn",
    )(c3, c3, cmp3, c3, c3, _nsa_sel_table(s), _nsa_cmp_table(ncmp))


def _sb_kernel(q_ref, k_ref, v_ref, o_ref, run_ref, acc_ref):
    i = pl.program_id(1)
    t0 = i * SB_TQ
    row = lax.broadcasted_iota(jnp.int32, (SB_TQ, TILE), 0)
    col = lax.broadcasted_iota(jnp.int32, (SB_TQ, TILE), 1)
    urow = lax.broadcasted_iota(jnp.int32, (2 * TILE, 2 * TILE), 0) & (TILE - 1)
    ucol = lax.broadcasted_iota(jnp.int32, (2 * TILE, 2 * TILE), 1)
    umat = jnp.where((ucol >= TILE) | (urow > ucol), 1.0, 0.0).astype(BF16)
    run_ref[...] = jnp.zeros_like(run_ref)
    acc_ref[...] = jnp.zeros_like(acc_ref)

    def tile(j, masked):
        ks = pl.multiple_of(j * TILE, TILE)
        if masked:
            strict = (col - row) < (t0 - ks)
        kt = k_ref[0, pl.ds(ks, TILE), :]
        vt = v_ref[0, pl.ds(ks, TILE), :]
        hss = [slice(h * HEAD_DIM, (h + 1) * HEAD_DIM) for h in range(N_HEADS)]
        zs = [_dot_nt(q_ref[0, :, hs], kt[:, hs]) for hs in hss]
        lss, css = [], []
        for z in zs:
            ls = jnp.minimum(z, 0.0) - jnp.log(1.0 + jnp.exp(-jnp.abs(z)))
            l1m = ls - z
            if masked:
                l1m = jnp.where(strict, l1m, 0.0)
            hi = l1m.astype(BF16)
            lo = (l1m - hi.astype(F32)).astype(BF16)
            lss.append(ls)
            css.append(_dot(jnp.concatenate([hi, lo], axis=-1), umat))
        for h in range(N_HEADS):
            run = run_ref[h]
            a = jnp.exp(lss[h] + run + css[h][:, 0:TILE])
            if masked:
                a = jnp.where(strict, a, 0.0)
            acc_ref[h] += _dot(a.astype(BF16), vt[:, hss[h]])
            run_ref[h] = run + css[h][:, TILE:2 * TILE]

    def run_max():
        return jnp.max(jnp.maximum(jnp.maximum(run_ref[0], run_ref[1]), jnp.maximum(run_ref[2], run_ref[3])))

    j_top = t0 // TILE + SB_TQ // TILE - 1
    for dj in range(SB_TQ // TILE):
        tile(j_top - dj, True)

    def cond(c):
        return (c[0] >= 0) & (c[1] > SB_UNDERFLOW)

    def body(c):
        tile(c[0], False)
        return c[0] - 1, run_max()

    lax.while_loop(cond, body, (j_top - SB_TQ // TILE, run_max()))
    o_ref[0] = jnp.concatenate([acc_ref[h] for h in range(N_HEADS)], axis=-1).astype(BF16)


def _sb_attention(c3):
    b, s, _ = c3.shape
    return pl.pallas_call(
        _sb_kernel,
        grid=(b, s // SB_TQ),
        in_specs=[pl.BlockSpec((1, SB_TQ, BRANCH_WIDTH), lambda bi, i: (bi, i, C_DQ // BRANCH_WIDTH)),
                  pl.BlockSpec((1, s, BRANCH_WIDTH), lambda bi, i: (bi, 0, C_DK // BRANCH_WIDTH)),
                  pl.BlockSpec((1, s, BRANCH_WIDTH), lambda bi, i: (bi, 0, C_DV // BRANCH_WIDTH))],
        out_specs=pl.BlockSpec((1, SB_TQ, BRANCH_WIDTH), lambda bi, i: (bi, i, 0)),
        out_shape=jax.ShapeDtypeStruct((b, s, BRANCH_WIDTH), BF16),
        scratch_shapes=[pltpu.VMEM((N_HEADS, SB_TQ, TILE), F32), pltpu.VMEM((N_HEADS, SB_TQ, HEAD_DIM), F32)],
        compiler_params=_cparams(("arbitrary", "arbitrary")),
        name="stickbreak",
    )(c3, c3, c3)


def _merge_kernel(x_ref, ng_ref, o1_ref, o2_ref, o3_ref, l1_ref, l2_ref, l3_ref, ga_ref,
                  yb_ref, gb_ref, yc_ref, gc_ref, yd_ref, gd_ref,
                  wm_ref, bm_ref, wb_ref, wo_ref, fg_ref, out_ref, *scratch, final):
    x = x_ref[...]
    ms = jnp.mean(x * x, axis=-1, keepdims=True)
    h = (x * lax.rsqrt(ms + RMS_EPS) * ng_ref[...]).astype(BF16)

    def token_order(ref, scr, dil):
        rows = x.shape[0] // dil
        for r in range(dil):
            for jl in range(BRANCH_WIDTH // LANE):
                lo = r * BRANCH_WIDTH + jl * LANE
                scr[jl, pl.ds(r, rows, stride=dil), :] = ref[:, lo:lo + LANE].astype(F32)
        return jnp.concatenate([scr[jl] for jl in range(BRANCH_WIDTH // LANE)], axis=-1)

    o2, l2 = token_order(o2_ref, scratch[0], 4), token_order(l2_ref, scratch[1], 4)
    o3, l3 = token_order(o3_ref, scratch[2], 16), token_order(l3_ref, scratch[3], 16)
    l1 = l1_ref[...]
    lm = jnp.maximum(jnp.maximum(l1, l2), l3)
    e1, e2, e3 = jnp.exp(l1 - lm), jnp.exp(l2 - lm), jnp.exp(l3 - lm)
    ya = (e1 * o1_ref[...].astype(F32) + e2 * o2 + e3 * o3) / (e1 + e2 + e3)
    ys = (ya, yb_ref[...].astype(F32), yc_ref[...].astype(F32), yd_ref[...].astype(F32))
    gs = (ga_ref, gb_ref, gc_ref, gd_ref)
    merged = None
    for i in range(4):
        y = (ys[i] * jax.nn.silu(gs[i][...].astype(F32))).astype(BF16)
        gate = jax.nn.sigmoid(_dot(h, wm_ref[i]) + bm_ref[i])
        term = gate * _dot(y, wb_ref[i])
        merged = term if merged is None else merged + term
    xn = x + _dot(merged.astype(BF16), wo_ref[...])
    if final:
        ms = jnp.mean(xn * xn, axis=-1, keepdims=True)
        xn = xn * lax.rsqrt(ms + RMS_EPS) * fg_ref[...]
    out_ref[...] = xn


def _merge(x2, ng, a_outs, a_lses, c2, yb, yc, yd, wm, bm, wb, wo, fg, final):
    n, d = x2.shape
    tm = 256
    bw = BRANCH_WIDTH

    def rows(width):
        return pl.BlockSpec((tm, width), lambda i: (i, 0))

    def cspec(off):
        return pl.BlockSpec((tm, bw), lambda i: (i, off // bw))

    def dspec(dil):
        return pl.BlockSpec((tm // dil, dil * bw), lambda i: (i, 0))

    return pl.pallas_call(
        functools.partial(_merge_kernel, final=final),
        grid=(n // tm,),
        in_specs=[rows(d), _const_spec((1, d)),
                  dspec(1), dspec(4), dspec(16), dspec(1), dspec(4), dspec(16), cspec(C_AGATE),
                  rows(bw), cspec(C_BGATE), rows(bw), cspec(C_CGATE), rows(bw), cspec(C_DGATE),
                  _const_spec((4, d, d)), _const_spec((4, 1, d)), _const_spec((4, bw, d)),
                  _const_spec((d, d)), _const_spec((1, d))],
        out_specs=rows(d),
        out_shape=jax.ShapeDtypeStruct((n, d), F32),
        scratch_shapes=[pltpu.VMEM((bw // LANE, tm, LANE), F32) for _ in range(4)],
        compiler_params=_cparams(("arbitrary",)),
        name="merge",
    )(x2, ng, *a_outs, *a_lses, c2, yb, c2, yc, c2, yd, c2, wm, bm, wb, wo, fg)


def _split_w_in(w):
    widths = (256, 256, 256, 256, MLA_Q_RANK, MLA_KV_RANK, MLA_ROPE_DIM, 256,
              256, 64, 64, 64, 64, 64, 64, 3 * N_HEADS, 256, 256, 256, 256, 256)
    names = ("a_q", "a_k", "a_v", "a_gate", "b_cq", "b_ckv", "b_kpe", "b_gate",
             "c_q", "c_kc", "c_vc", "c_ks", "c_vs", "c_kw", "c_vw", "c_g", "c_gate",
             "d_q", "d_k", "d_v", "d_gate")
    out, off = {}, 0
    for nm, wd in zip(names, widths):
        out[nm] = w[:, off:off + wd]
        off += wd
    return out


def _rot_cols(w):
    half = w.shape[1] // 2
    return jnp.concatenate([-w[:, half:], w[:, :half]], axis=1)


def _prep_w_in(w):
    p = _split_w_in(w)
    d = w.shape[0]
    qs = HEAD_DIM ** -0.5

    def z(n):
        return jnp.zeros((d, n), w.dtype)

    kpe_blk = jnp.concatenate([z(ROPE_LANE), p["b_kpe"], z(LANE - ROPE_LANE - MLA_ROPE_DIM)], axis=1)
    kper_blk = jnp.concatenate([z(ROPE_LANE), _rot_cols(p["b_kpe"]), z(LANE - ROPE_LANE - MLA_ROPE_DIM)], axis=1)
    cols = [p["a_q"] * qs, p["a_k"], p["a_v"],
            p["a_gate"], p["b_cq"], z(256 - MLA_Q_RANK), p["b_ckv"], kpe_blk, kper_blk,
            p["c_g"], z(128 - 3 * N_HEADS), p["b_gate"],
            p["c_q"] * qs, p["c_ks"], p["c_vs"], p["c_kw"], p["c_vw"], p["c_gate"],
            p["d_q"] * qs, p["d_k"], p["d_v"], p["d_gate"],
            p["c_kc"], p["c_vc"]]
    out = jnp.concatenate(cols, axis=1)
    assert out.shape[1] == W_WIDTH
    return out.astype(BF16)


def _prep_mla(w_uq, w_ukv, gq):
    qd = MLA_NOPE_DIM + MLA_ROPE_DIM
    pad = LANE - qd
    wl, wr, wk, wv = [], [], [], []
    for h in range(N_HEADS):
        nope = w_uq[:, h * qd:h * qd + MLA_NOPE_DIM]
        rp = w_uq[:, h * qd + MLA_NOPE_DIM:(h + 1) * qd]
        zq = jnp.zeros((MLA_Q_RANK, pad), w_uq.dtype)
        wl += [nope, rp, zq]
        wr += [jnp.zeros_like(nope), _rot_cols(rp), zq]
        kn = w_ukv[:, h * 128:h * 128 + MLA_NOPE_DIM]
        wk += [kn, jnp.zeros((MLA_KV_RANK, LANE - MLA_NOPE_DIM), w_ukv.dtype)]
        wv += [w_ukv[:, h * 128 + MLA_NOPE_DIM:(h + 1) * 128],
               jnp.zeros((MLA_KV_RANK, LANE - HEAD_DIM), w_ukv.dtype)]
    rpad = ((0, 256 - MLA_Q_RANK), (0, 0))
    wl = jnp.pad(jnp.concatenate(wl, axis=1), rpad).astype(BF16)
    wr = jnp.pad(jnp.concatenate(wr, axis=1), rpad).astype(BF16)
    gq = jnp.pad(gq, (0, 256 - MLA_Q_RANK)).reshape(1, 256)
    return wl, wr, jnp.concatenate(wk, axis=1).astype(BF16), jnp.concatenate(wv, axis=1).astype(BF16), gq


def _rope_tables(s):
    half = MLA_ROPE_DIM // 2
    inv = ROPE_THETA ** (-jnp.arange(half, dtype=F32) / half)
    ang = jnp.arange(s, dtype=F32)[:, None] * inv[None, :]
    cos, sin = jnp.cos(ang), jnp.sin(ang)
    ones = jnp.ones((s, ROPE_LANE), F32)
    zl = jnp.zeros((s, ROPE_LANE), F32)
    zr = jnp.zeros((s, LANE - ROPE_LANE - MLA_ROPE_DIM), F32)
    return (jnp.concatenate([ones, cos, cos, zr], axis=1), jnp.concatenate([zl, sin, sin, zr], axis=1))


def _prep_nsa(pos, w1, b1, w2, b2):
    half = NSA_CMP_BLOCK // 2
    eye = jnp.eye(2, dtype=w1.dtype)
    w1r = w1.reshape(2, NSA_CMP_BLOCK, NSA_KV_DIM, NSA_CMP_HIDDEN)
    posr = pos

    def comb_w(part):
        return jnp.einsum("ktdh,kj->tkdjh", part, eye).reshape(half * 2 * NSA_KV_DIM, 2 * NSA_CMP_HIDDEN)

    def comb_p(part):
        flat = jnp.transpose(part, (1, 0, 2)).reshape(1, half * 2 * NSA_KV_DIM)
        return jnp.pad(flat, ((0, 7), (0, 0))).astype(BF16)

    wt, wb = comb_w(w1r[:, :half]).astype(BF16), comb_w(w1r[:, half:]).astype(BF16)
    pt, pb = comb_p(posr[:, :half]), comb_p(posr[:, half:])
    b1c = b1.reshape(1, 2 * NSA_CMP_HIDDEN)
    zero = jnp.zeros((NSA_CMP_HIDDEN, NSA_KV_DIM), w2.dtype)
    w2c = jnp.concatenate([jnp.concatenate([w2[0], zero], axis=1),
                           jnp.concatenate([zero, w2[1]], axis=1)], axis=0).astype(BF16)
    b2c = b2.reshape(1, 2 * NSA_KV_DIM)
    return pt, pb, wt, wb, b1c, w2c, b2c


def kernel(x, norm_g, w_in, mla_q_norm, mla_w_uq, mla_kv_norm, mla_w_ukv, nsa_pos, nsa_w1, nsa_b1,
           nsa_w2, nsa_b2, w_branch, w_merge, b_merge, w_out, final_norm_g):
    b, s, d = x.shape
    depth = norm_g.shape[0]
    n = b * s
    assert s % (TILE * max(DSW_DILATIONS)) == 0 and s >= NSA_WINDOW + TILE
    cos_t, sin_t = _rope_tables(s)
    x2 = x.reshape(n, d)
    for layer in range(depth):
        *a_cls, c2, kvc = _inproj(x2, norm_g[layer].reshape(1, d), _prep_w_in(w_in[layer]))
        c3 = c2.reshape(b, s, C_WIDTH)
        a_res = [_dsw_attention(a.reshape(b, s // dil, dil * A_WIDTH), dil) for a, dil in zip(a_cls, DSW_DILATIONS)]
        wl, wr, wk, wv, gq = _prep_mla(mla_w_uq[layer], mla_w_ukv[layer], mla_q_norm[layer])
        qb, kb, vb = _mla_prep(c2, s, cos_t, sin_t, gq, wl, wr, mla_kv_norm[layer].reshape(1, MLA_KV_RANK), wk, wv)
        hq = N_HEADS * LANE
        yb = _mla_flash(qb.reshape(b, s, hq), kb.reshape(b, s, hq), vb.reshape(b, s, hq))
        cmp3 = _nsa_compress(kvc.reshape(b, s, 128),
                             *_prep_nsa(nsa_pos[layer], nsa_w1[layer], nsa_b1[layer], nsa_w2[layer], nsa_b2[layer]))
        yc = _nsa_attention(c3, cmp3)
        yd = _sb_attention(c3)
        x2 = _merge(x2, norm_g[layer].reshape(1, d), [r[0] for r in a_res], [r[1] for r in a_res], c2,
                    yb.reshape(n, BRANCH_WIDTH), yc.reshape(n, BRANCH_WIDTH), yd.reshape(n, BRANCH_WIDTH),
                    w_merge[layer].astype(BF16), b_merge[layer].reshape(4, 1, d),
                    w_branch[layer].astype(BF16), w_out[layer].astype(BF16),
                    final_norm_g.reshape(1, d), final=(layer == depth - 1))
    return x2.reshape(b, s, d)
```
